```python
import math
import jax
import jax.numpy as jnp
from jax import lax
import numpy as np

D_MODEL = 1024
BATCH = 8
SEQ = 8192
DEPTH = 4

GRID_W = 64
CTX_LEN = 256
N_MIXERS = 3
N_LAYERS_A = (DEPTH + 2) // 3
N_LAYERS_B = (DEPTH + 1) // 3
N_LAYERS_C = DEPTH // 3
Q_BLOCK = 128
ROPE_BASE = 10000.0
NORM_EPS = 1e-6
FFN_HIDDEN = ((8 * D_MODEL + 3 * 256 - 1) // (3 * 256)) * 256

DA_HEAD_DIM = 64
DA_HEADS = D_MODEL // (2 * DA_HEAD_DIM)
DA_V_DIM = 2 * DA_HEAD_DIM
DA_SUBLN_EPS = 1e-5
RW_HEAD = 64
RW_HEADS = D_MODEL // RW_HEAD
RW_DECAY_LORA = 64
RW_AAA_LORA = 64
RW_GATE_LORA = 160
RW_GN_EPS = 64e-5
MLA_HEADS = 16
MLA_Q_LORA = 256
MLA_KV_LORA = 128
MLA_NOPE = 64
MLA_ROPE = 32
MLA_V = 64

kernel_name = 'hybrid_diffattn_rwkv7_mla_prefix_dit'


def _rms_norm(x, g, eps=NORM_EPS):
    xf = x.astype(jnp.float32)
    y = xf * lax.rsqrt(jnp.mean(xf * xf, axis=-1, keepdims=True) + eps)
    return (y * g.astype(jnp.float32)).astype(x.dtype)


def _modulate(h, shift, scale):
    return h * (1 + scale) + shift


def _swiglu(h, w1, w3, w2):
    return (jax.nn.silu(h @ w1) * (h @ w3)) @ w2


def _axial_angles(n, rot_dim):
    rows = n // GRID_W
    row = jnp.repeat(jnp.arange(rows, dtype=jnp.float32), GRID_W)
    col = jnp.tile(jnp.arange(GRID_W, dtype=jnp.float32), rows)
    m = rot_dim // 4
    inv = ROPE_BASE ** (-jnp.arange(m, dtype=jnp.float32) / m)
    return row[:, None] * inv, col[:, None] * inv


def _rotate(x, ang):
    cos = jnp.cos(ang)[:, None, :]
    sin = jnp.sin(ang)[:, None, :]
    x1, x2 = jnp.split(x.astype(jnp.float32), 2, axis=-1)
    return jnp.concatenate([x1 * cos - x2 * sin, x2 * cos + x1 * sin], axis=-1)


def _axial_rope(x, angles):
    ang_row, ang_col = angles
    half = x.shape[-1] // 2
    y = jnp.concatenate([_rotate(x[..., :half], ang_row), _rotate(x[..., half:], ang_col)], axis=-1)
    return y.astype(x.dtype)


def _sweep_query_blocks(attend, *qs):
    b, n = qs[0].shape[:2]
    nb = n // Q_BLOCK
    blocks = tuple(jnp.moveaxis(q.reshape((b, nb, Q_BLOCK) + q.shape[2:]), 1, 0) for q in qs)
    out = lax.map(lambda blk: attend(*blk), blocks)
    out = jnp.moveaxis(out, 0, 1)
    return out.reshape((b, n) + out.shape[3:])


def _diff_attend(q, k, v, lam):
    s = jnp.einsum('bqhmd,bkhmd->bhmqk', q, k, preferred_element_type=jnp.float32) * (DA_HEAD_DIM ** -0.5)
    p = jax.nn.softmax(s, axis=-1)
    w = p[:, :, 0] - lam * p[:, :, 1]
    return jnp.einsum('bhqk,bkhe->bqhe', w.astype(v.dtype), v)


def _diff_attention(h_lat, h_ctx, wqkv, lam_vecs, subln_g, wo, layer_idx, need_ctx):
    lam_init = 0.8 - 0.6 * math.exp(-0.3 * layer_idx)
    lv = lam_vecs.astype(jnp.float32)
    lam = jnp.exp(jnp.sum(lv[0] * lv[1])) - jnp.exp(jnp.sum(lv[2] * lv[3])) + lam_init

    def project(h):
        b, t, _ = h.shape
        q, k, v = jnp.split(h @ wqkv, 3, axis=-1)
        return (q.reshape(b, t, 2 * DA_HEADS, DA_HEAD_DIM), k.reshape(b, t, 2 * DA_HEADS, DA_HEAD_DIM),
                v.reshape(b, t, DA_HEADS, DA_V_DIM))

    def pairs(t):
        return t.reshape(t.shape[0], t.shape[1], DA_HEADS, 2, DA_HEAD_DIM)

    def finish(o):
        b, t = o.shape[:2]
        o = _rms_norm(o, subln_g, DA_SUBLN_EPS) * (1 - lam_init)
        return o.reshape(b, t, DA_HEADS * DA_V_DIM) @ wo

    angles = _axial_angles(h_lat.shape[1], DA_HEAD_DIM)
    q_l, k_l, v_l = project(h_lat)
    q_l = pairs(_axial_rope(q_l, angles))
    k_l = pairs(_axial_rope(k_l, angles))
    q_c, k_c, v_c = project(h_ctx)
    q_c, k_c = pairs(q_c), pairs(k_c)
    k_all = jnp.concatenate([k_c, k_l], axis=1)
    v_all = jnp.concatenate([v_c, v_l], axis=1)
    y_lat = finish(_sweep_query_blocks(lambda qb: _diff_attend(qb, k_all, v_all, lam), q_l))
    y_ctx = finish(_diff_attend(q_c, k_c, v_c, lam)) if need_ctx else None
    return y_lat, y_ctx


def _centred_shift(x):
    pad = jnp.pad(x, ((0, 0), (1, 1), (0, 0)))
    return 0.5 * (pad[:, :-2] + pad[:, 2:]) - x


def _rwkv_features(h, mu, wrkv, w0, w1, w2, a0, a1, a2, g1, g2, k_k, k_a):
    b, t, d = h.shape
    xs = h[None] + _centred_shift(h)[None] * mu[:, None, None, :]
    r, k, v = jnp.einsum('sbtd,sde->sbte', xs[:3], wrkv)
    x_w, x_a, x_g = xs[3], xs[4], xs[5]
    w_log = -jax.nn.softplus(-(w0[:, None, None, :] + jnp.einsum(
        'zbtl,zld->zbtd', jnp.tanh(jnp.einsum('btd,zdl->zbtl', x_w, w1)), w2))) - 0.5
    decay = jnp.exp(-jnp.exp(w_log.astype(jnp.float32)))
    a = jax.nn.sigmoid(a0[:, None, None, :] + jnp.einsum('zbtl,zld->zbtd', jnp.einsum('btd,zdl->zbtl', x_a, a1), a2))
    g = jax.nn.sigmoid(x_g @ g1) @ g2
    kk = (k * k_k).astype(jnp.float32).reshape(b, t, RW_HEADS, RW_HEAD)
    kk = (kk / jnp.maximum(jnp.linalg.norm(kk, axis=-1, keepdims=True), 1e-12)).reshape(b, t, d)
    k_dir = k[None] * (1 + (a - 1) * k_a)
    b_dir = kk[None] * a
    return r, k_dir, v, decay, -kk, b_dir, g


def _wkv_scan(state, r, decay, k, v, a, b, reverse, emit):
    bsz, t, d = r.shape

    def time_major(z):
        return jnp.moveaxis(z.astype(jnp.float32).reshape(bsz, t, RW_HEADS, RW_HEAD), 1, 0)

    def step(s, inp):
        r_t, w_t, k_t, v_t, a_t, b_t = inp
        sa = jnp.einsum('bhvk,bhk->bhv', s, a_t)
        s = s * w_t[:, :, None, :] + sa[..., None] * b_t[:, :, None, :] + v_t[..., None] * k_t[:, :, None, :]
        y = jnp.einsum('bhvk,bhk->bhv', s, r_t) if emit else None
        return s, y

    s, ys = lax.scan(step, state, tuple(time_major(z) for z in (r, decay, k, v, a, b)), reverse=reverse)
    if not emit:
        return s, None
    return s, jnp.moveaxis(ys, 0, 1).reshape(bsz, t, d)


def _rwkv7(h_lat, h_ctx, mu, wrkv, w0, w1, w2, a0, a1, a2, g1, g2, k_k, k_a, r_k, lnx_g, lnx_b, wo, need_ctx):
    f_ctx = _rwkv_features(h_ctx, mu, wrkv, w0, w1, w2, a0, a1, a2, g1, g2, k_k, k_a)
    f_lat = _rwkv_features(h_lat, mu, wrkv, w0, w1, w2, a0, a1, a2, g1, g2, k_k, k_a)
    s0 = jnp.zeros((h_lat.shape[0], RW_HEADS, RW_HEAD, RW_HEAD), jnp.float32)

    def run(f, states, emit):
        r, k_dir, v, decay, a_vec, b_dir, _ = f
        finals, ys = [], []
        for dr in range(2):
            s, y = _wkv_scan(states[dr], r, decay[dr], k_dir[dr], v, a_vec, b_dir[dr], dr == 1, emit)
            finals.append(s)
            ys.append(y)
        return finals, ys

    def finish(f, ys):
        r, k_dir, v, _, _, _, g = f
        b, t, d = r.shape
        y = (ys[0] + ys[1]).reshape(b, t, RW_HEADS, RW_HEAD)
        mean = jnp.mean(y, axis=-1, keepdims=True)
        var = jnp.mean(jnp.square(y - mean), axis=-1, keepdims=True)
        y = ((y - mean) * lax.rsqrt(var + RW_GN_EPS)).reshape(b, t, d) * lnx_g + lnx_b
        rk = (r * (k_dir[0] + k_dir[1])).reshape(b, t, RW_HEADS, RW_HEAD) * r_k
        bonus = jnp.sum(rk, axis=-1, keepdims=True) * v.reshape(b, t, RW_HEADS, RW_HEAD)
        y = y + bonus.reshape(b, t, d)
        return (y * g).astype(v.dtype) @ wo

    ctx_states, ys_ctx = run(f_ctx, (s0, s0), need_ctx)
    _, ys_lat = run(f_lat, ctx_states, True)
    y_lat = finish(f_lat, ys_lat)
    y_ctx = finish(f_ctx, ys_ctx) if need_ctx else None
    return y_lat, y_ctx


def _mla_attend(qn, qr, kn, kr, v):
    s = (jnp.einsum('bqhd,bkhd->bhqk', qn, kn, preferred_element_type=jnp.float32)
         + jnp.einsum('bqhr,bkr->bhqk', qr, kr, preferred_element_type=jnp.float32)) * ((MLA_NOPE + MLA_ROPE) ** -0.5)
    p = jax.nn.softmax(s, axis=-1)
    return jnp.einsum('bhqk,bkhe->bqhe', p.astype(v.dtype), v)


def _mla(h_lat, h_ctx, wdown, q_norm_g, wuq, kv_norm_g, wukv, wo, need_ctx):
    def project(h):
        b, t, _ = h.shape
        down = h @ wdown
        cq = _rms_norm(down[..., :MLA_Q_LORA], q_norm_g)
        ckv = _rms_norm(down[..., MLA_Q_LORA:MLA_Q_LORA + MLA_KV_LORA], kv_norm_g)
        k_rope = down[..., MLA_Q_LORA + MLA_KV_LORA:].reshape(b, t, 1, MLA_ROPE)
        q = (cq @ wuq).reshape(b, t, MLA_HEADS, MLA_NOPE + MLA_ROPE)
        kv = (ckv @ wukv).reshape(b, t, MLA_HEADS, MLA_NOPE + MLA_V)
        return q[..., :MLA_NOPE], q[..., MLA_NOPE:], kv[..., :MLA_NOPE], k_rope, kv[..., MLA_NOPE:]

    def finish(o):
        b, t = o.shape[:2]
        return o.reshape(b, t, MLA_HEADS * MLA_V) @ wo

    angles = _axial_angles(h_lat.shape[1], MLA_ROPE)
    qn_l, qr_l, kn_l, kr_l, v_l = project(h_lat)
    qr_l = _axial_rope(qr_l, angles)
    kr_l = _axial_rope(kr_l, angles)[:, :, 0]
    qn_c, qr_c, kn_c, kr_c, v_c = project(h_ctx)
    kr_c = kr_c[:, :, 0]
    kn_all = jnp.concatenate([kn_c, kn_l], axis=1)
    kr_all = jnp.concatenate([kr_c, kr_l], axis=1)
    v_all = jnp.concatenate([v_c, v_l], axis=1)
    y_lat = finish(_sweep_query_blocks(lambda qn, qr: _mla_attend(qn, qr, kn_all, kr_all, v_all), qn_l, qr_l))
    y_ctx = finish(_mla_attend(qn_c, qr_c, kn_c, kr_c, v_c)) if need_ctx else None
    return y_lat, y_ctx


def setup_inputs(seed: int = 0) -> dict:
    key = jax.random.key(seed)
    ks = iter(jax.random.split(key, 64))
    D, F = D_MODEL, FFN_HIDDEN
    NA, NB, NC = N_LAYERS_A, N_LAYERS_B, N_LAYERS_C

    def nrm(shape, scale):
        return scale * jax.random.normal(next(ks), shape, jnp.float32)

    def gain(shape):
        return 1.0 + nrm(shape, 0.02)

    def unif(shape, lo, hi):
        return jax.random.uniform(next(ks), shape, jnp.float32, lo, hi)

    return {
        'x': nrm((BATCH, SEQ, D), 1.0),
        'c': nrm((BATCH, D), 1.0),
        'ctx': nrm((BATCH, CTX_LEN, D), 1.0),
        'c_ctx': nrm((D,), 1.0),
        'ada_w': nrm((DEPTH, D, 6 * D), 0.5 * D ** -0.5),
        'ada_b': nrm((DEPTH, 6 * D), 0.02),
        'norm1_g': gain((DEPTH, D)),
        'norm2_g': gain((DEPTH, D)),
        'ffn_w1': nrm((DEPTH, D, F), D ** -0.5),
        'ffn_w3': nrm((DEPTH, D, F), D ** -0.5),
        'ffn_w2': nrm((DEPTH, F, D), F ** -0.5),
        'final_norm_g': gain((D,)),
        'da_wqkv': nrm((NA, D, 3 * D), D ** -0.5),
        'da_lambda': nrm((NA, 4, DA_HEAD_DIM), 0.1),
        'da_subln_g': gain((NA, DA_V_DIM)),
        'da_wo': nrm((NA, DA_HEADS * DA_V_DIM, D), (DA_HEADS * DA_V_DIM) ** -0.5),
        'rw_mu': unif((NB, 6, D), 0.0, 1.0),
        'rw_wrkv': nrm((NB, 3, D, D), D ** -0.5),
        'rw_w0': unif((NB, 2, D), -6.0, 0.0),
        'rw_w1': nrm((NB, 2, D, RW_DECAY_LORA), 0.1 * D ** -0.5),
        'rw_w2': nrm((NB, 2, RW_DECAY_LORA, D), 0.1 * RW_DECAY_LORA ** -0.5),
        'rw_a0': nrm((NB, 2, D), 0.1),
        'rw_a1': nrm((NB, 2, D, RW_AAA_LORA), 0.1 * D ** -0.5),
        'rw_a2': nrm((NB, 2, RW_AAA_LORA, D), 0.1 * RW_AAA_LORA ** -0.5),
        'rw_g1': nrm((NB, D, RW_GATE_LORA), D ** -0.5),
        'rw_g2': nrm((NB, RW_GATE_LORA, D), RW_GATE_LORA ** -0.5),
        'rw_k_k': 0.85 + nrm((NB, D), 0.02),
        'rw_k_a': gain((NB, D)),
        'rw_r_k': nrm((NB, RW_HEADS, RW_HEAD), 0.1),
        'rw_lnx_g': gain((NB, D)),
        'rw_lnx_b': nrm((NB, D), 0.02),
        'rw_wo': nrm((NB, D, D), D ** -0.5),
        'mla_wdown': nrm((NC, D, MLA_Q_LORA + MLA_KV_LORA + MLA_ROPE), D ** -0.5),
        'mla_q_norm_g': gain((NC, MLA_Q_LORA)),
        'mla_wuq': nrm((NC, MLA_Q_LORA, MLA_HEADS * (MLA_NOPE + MLA_ROPE)), MLA_Q_LORA ** -0.5),
        'mla_kv_norm_g': gain((NC, MLA_KV_LORA)),
        'mla_wukv': nrm((NC, MLA_KV_LORA, MLA_HEADS * (MLA_NOPE + MLA_V)), MLA_KV_LORA ** -0.5),
        'mla_wo': nrm((NC, MLA_HEADS * MLA_V, D), (MLA_HEADS * MLA_V) ** -0.5),
    }


def reference(x, c, ctx, c_ctx, ada_w, ada_b, norm1_g, norm2_g, ffn_w1, ffn_w3, ffn_w2, final_norm_g,
              da_wqkv, da_lambda, da_subln_g, da_wo,
              rw_mu, rw_wrkv, rw_w0, rw_w1, rw_w2, rw_a0, rw_a1, rw_a2, rw_g1, rw_g2,
              rw_k_k, rw_k_a, rw_r_k, rw_lnx_g, rw_lnx_b, rw_wo,
              mla_wdown, mla_q_norm_g, mla_wuq, mla_kv_norm_g, mla_wukv, mla_wo):
    b = x.shape[0]
    cond_lat = jax.nn.silu(c)
    cond_ctx = jax.nn.silu(c_ctx)
    for i in range(DEPTH):
        need_ctx = i < DEPTH - 1
        mod_lat = (cond_lat @ ada_w[i] + ada_b[i]).reshape(b, 1, 6, D_MODEL)
        mod_ctx = (cond_ctx @ ada_w[i] + ada_b[i]).reshape(6, D_MODEL)
        h_lat = _modulate(_rms_norm(x, norm1_g[i]), mod_lat[..., 0, :], mod_lat[..., 1, :])
        h_ctx = _modulate(_rms_norm(ctx, norm1_g[i]), mod_ctx[..., 0, :], mod_ctx[..., 1, :])
        kind, j = i % N_MIXERS, i // N_MIXERS
        if kind == 0:
            y_lat, y_ctx = _diff_attention(h_lat, h_ctx, da_wqkv[j], da_lambda[j], da_subln_g[j], da_wo[j], i, need_ctx)
        elif kind == 1:
            y_lat, y_ctx = _rwkv7(h_lat, h_ctx, rw_mu[j], rw_wrkv[j], rw_w0[j], rw_w1[j], rw_w2[j],
                                  rw_a0[j], rw_a1[j], rw_a2[j], rw_g1[j], rw_g2[j], rw_k_k[j], rw_k_a[j],
                                  rw_r_k[j], rw_lnx_g[j], rw_lnx_b[j], rw_wo[j], need_ctx)
        else:
            y_lat, y_ctx = _mla(h_lat, h_ctx, mla_wdown[j], mla_q_norm_g[j], mla_wuq[j], mla_kv_norm_g[j],
                                mla_wukv[j], mla_wo[j], need_ctx)
        x = x + mod_lat[..., 2, :] * y_lat
        h = _modulate(_rms_norm(x, norm2_g[i]), mod_lat[..., 3, :], mod_lat[..., 4, :])
        x = x + mod_lat[..., 5, :] * _swiglu(h, ffn_w1[i], ffn_w3[i], ffn_w2[i])
        if need_ctx:
            ctx = ctx + mod_ctx[..., 2, :] * y_ctx
            hc = _modulate(_rms_norm(ctx, norm2_g[i]), mod_ctx[..., 3, :], mod_ctx[..., 4, :])
            ctx = ctx + mod_ctx[..., 5, :] * _swiglu(hc, ffn_w1[i], ffn_w3[i], ffn_w2[i])
    return _rms_norm(x, final_norm_g)
```

```python
import functools
import math

import jax
import jax.numpy as jnp
from jax import lax
from jax.experimental import pallas as pl
from jax.experimental.pallas import tpu as pltpu

F32 = jnp.float32
BF16 = jnp.bfloat16

GRID_W = 64
ROPE_BASE = 10000.0
NORM_EPS = 1e-6
N_MIXERS = 3
DA_HEAD_DIM = 64
DA_SUBLN_EPS = 1e-5
RW_HEAD = 64
RW_GN_EPS = 64e-5
MLA_HEADS = 16
MLA_Q_LORA = 256
MLA_KV_LORA = 128
MLA_NOPE = 64
MLA_ROPE = 32
MLA_V = 64
LOG2E = math.log2(math.e)

TM = 256
KV_TILES = 3
SCAN_TC = 32
ADA_TN = 1536
VMEM_LIMIT = 56 * 1024 * 1024


def _cparams(n_grid):
    return pltpu.CompilerParams(dimension_semantics=("arbitrary",) * n_grid, vmem_limit_bytes=VMEM_LIMIT)


def _const_spec(shape):
    nd = len(shape)
    return pl.BlockSpec(shape, lambda *_: (0,) * nd)


def _norm_mod(x, g, shift, scale):
    ms = jnp.mean(x * x, axis=-1, keepdims=True)
    return (x * lax.rsqrt(ms + NORM_EPS) * g) * (1.0 + scale) + shift


def _dot(a, b):
    return jnp.dot(a, b, preferred_element_type=F32)


def _dot_nt(a, b):
    return lax.dot_general(a, b, (((1,), (1,)), ((), ())), preferred_element_type=F32)


def _ada_body(c_ref, w_ref, b_ref, o_ref):
    c = c_ref[...]
    s = (c * jax.nn.sigmoid(c)).astype(BF16)
    o_ref[0] = _dot(s, w_ref[0].astype(BF16)) + b_ref[0]


def _ada_call(cond, ada_w, ada_b):
    L, D, D6 = ada_w.shape
    R = cond.shape[0]
    return pl.pallas_call(
        _ada_body,
        grid=(L, D6 // ADA_TN),
        in_specs=[
            pl.BlockSpec((R, D), lambda l, j: (0, 0)),
            pl.BlockSpec((1, D, ADA_TN), lambda l, j: (l, 0, j)),
            pl.BlockSpec((1, 1, ADA_TN), lambda l, j: (l, 0, j)),
        ],
        out_specs=pl.BlockSpec((1, R, ADA_TN), lambda l, j: (l, 0, j)),
        out_shape=jax.ShapeDtypeStruct((L, R, D6), F32),
        compiler_params=_cparams(2),
        name="ada_mod",
    )(cond, ada_w, ada_b.reshape(L, 1, D6))


def _tok_specs(B, nct, D):
    x_spec = pl.BlockSpec((1, TM, D), lambda b, t: (b, t, 0))
    mod_spec = pl.BlockSpec((1, 6, D), lambda b, t: (jnp.where(t < nct, B, b), 0, 0))
    return x_spec, mod_spec


def _da_pre_body(x_ref, mod_ref, g_ref, wqt_ref, wk_ref, wks_ref, wvt_ref, cq_ref, sq_ref, ck_ref, sk_ref,
                 qt_ref, k_ref, vt_ref):
    m = mod_ref[0]
    hb = _norm_mod(x_ref[0], g_ref[...], m[0:1], m[1:2]).astype(BF16)
    D = hb.shape[1]
    qt = _dot_nt(wqt_ref[...], hb)
    cq = cq_ref[...]
    sq = sq_ref[...]
    hd = DA_HEAD_DIM
    for h in range(D // hd):
        blk = [qt[h * hd + 16 * i:h * hd + 16 * (i + 1)] for i in range(4)]
        for i in range(4):
            r0 = 16 * i
            out = blk[i] * cq[r0:r0 + 16] + blk[i ^ 1] * sq[r0:r0 + 16]
            qt_ref[0, 0, h * hd + r0:h * hd + r0 + 16, :] = out.astype(BF16)
    k = _dot(hb, wk_ref[...])
    ks = _dot(hb, wks_ref[...])
    ck = ck_ref[...]
    sk = sk_ref[...]
    for j in range(D // 128):
        sl = slice(128 * j, 128 * (j + 1))
        k_ref[0, :, sl] = (k[:, sl] * ck + ks[:, sl] * sk).astype(BF16)
    vt_ref[0, 0] = _dot_nt(wvt_ref[...], hb).astype(BF16)


def _da_pre_call(xa, mod_i, g1, wqt, wk, wks, wvt, cq, sq, ck, sk, nct):
    B, T, D = xa.shape
    NT = T // TM
    x_spec, mod_spec = _tok_specs(B, nct, D)
    fm_spec = pl.BlockSpec((1, 1, D, TM), lambda b, t: (b, t, 0, 0))
    return pl.pallas_call(
        _da_pre_body,
        grid=(B, NT),
        in_specs=[
            x_spec, mod_spec, _const_spec((1, D)),
            _const_spec((D, D)), _const_spec((D, D)), _const_spec((D, D)), _const_spec((D, D)),
            pl.BlockSpec((64, TM), lambda b, t: (0, t)), pl.BlockSpec((64, TM), lambda b, t: (0, t)),
            pl.BlockSpec((TM, 128), lambda b, t: (t, 0)), pl.BlockSpec((TM, 128), lambda b, t: (t, 0)),
        ],
        out_specs=[fm_spec, x_spec, fm_spec],
        out_shape=[
            jax.ShapeDtypeStruct((B, NT, D, TM), BF16),
            jax.ShapeDtypeStruct((B, T, D), BF16),
            jax.ShapeDtypeStruct((B, NT, D, TM), BF16),
        ],
        compiler_params=_cparams(2),
        name="da_pre",
    )(xa, mod_i, g1, wqt, wk, wks, wvt, cq, sq, ck, sk)


def _attn_body(lam_ref, g_ref, q_ref, k_ref, v_ref, o_ref, m_ref, l_ref, acc_ref, *, mode, nct, nt, lam_init):
    qi = pl.program_id(2)
    q = q_ref[0, 0].astype(F32)
    dk, tq = q.shape
    rows = lax.broadcasted_iota(jnp.int32, (dk, tq), 0)
    if mode == "diff":
        sel_a = rows < 64
        qa = jnp.where(sel_a, q, 0.0)
        qb = jnp.where(sel_a, 0.0, q)
    else:
        blk = rows // 32
        in_a = (blk == 0) | (blk == 1) | (blk == 4)
        in_b = (blk == 2) | (blk == 3) | (blk == 5)
        qa = jnp.where(in_a, q, 0.0)
        qb = jnp.where(in_b, q, 0.0)
    q2 = jnp.concatenate([qa, qb], axis=1).astype(BF16)

    def init():
        m_ref[...] = jnp.full(m_ref.shape, -1e30, F32)
        l_ref[...] = jnp.zeros(l_ref.shape, F32)
        acc_ref[...] = jnp.zeros(acc_ref.shape, F32)

    def chunk(c0, ntile):
        tk = ntile * TM
        k0 = c0 * TM if isinstance(c0, int) else pl.multiple_of(c0 * TM, TM)
        kc = k_ref[0, pl.ds(k0, tk), :]
        s = _dot(kc, q2)
        m_prev = m_ref[...]
        m_new = jnp.maximum(m_prev, jnp.max(s, axis=0, keepdims=True))
        alpha = jnp.exp2(m_prev - m_new)
        p = jnp.exp2(s - m_new)
        l_ref[...] = alpha * l_ref[...] + jnp.sum(p, axis=0, keepdims=True)
        pb = p.astype(BF16)
        pv = _dot(v_ref[0, c0], pb[0:TM])
        for j in range(1, ntile):
            pv = pv + _dot(v_ref[0, c0 + j], pb[j * TM:(j + 1) * TM])
        acc_ref[...] = alpha * acc_ref[...] + pv
        m_ref[...] = m_new

    def finish():
        a = acc_ref[...] * (1.0 / l_ref[...])
        if mode == "diff":
            lv = lam_ref[...]
            lam = (jnp.exp(jnp.sum(lv[0:1] * lv[1:2], keepdims=True))
                   - jnp.exp(jnp.sum(lv[2:3] * lv[3:4], keepdims=True)) + lam_init)
            o = a[:, :tq] - lam * a[:, tq:]
            ms = jnp.mean(o * o, axis=0, keepdims=True)
            o = o * lax.rsqrt(ms + DA_SUBLN_EPS)
            ot = (o.T * g_ref[...]) * (1.0 - lam_init)
        else:
            ot = jnp.concatenate([a[:64, :tq], a[64:, tq:]], axis=0).T
        o_ref[0] = ot.astype(BF16)

    @pl.when(qi < nct)
    def _():
        init()
        for c in range(nct):
            chunk(c, 1)
        finish()

    @pl.when(qi >= nct)
    def _():
        init()

        def step(i, carry):
            chunk(i * KV_TILES, KV_TILES)
            return carry

        lax.fori_loop(0, nt // KV_TILES, step, 0)
        finish()


def _attn_call(lam, g, qt, k, vt, *, mode, nct, lam_init):
    B, NT, GD, _ = qt.shape
    T = NT * TM
    G = vt.shape[2] // 128
    dk = GD // G
    body = functools.partial(_attn_body, mode=mode, nct=nct, nt=NT, lam_init=lam_init)
    return pl.pallas_call(
        body,
        grid=(B, G, NT),
        in_specs=[
            _const_spec(lam.shape), _const_spec(g.shape),
            pl.BlockSpec((1, 1, dk, TM), lambda b, h, q: (b, q, h, 0)),
            pl.BlockSpec((1, T, dk), lambda b, h, q: (b, 0, h)),
            pl.BlockSpec((1, NT, 128, TM), lambda b, h, q: (b, 0, h, 0)),
        ],
        out_specs=pl.BlockSpec((1, TM, 128), lambda b, h, q: (b, q, h)),
        out_shape=jax.ShapeDtypeStruct((B, T, G * 128), BF16),
        scratch_shapes=[
            pltpu.VMEM((1, 2 * TM), F32), pltpu.VMEM((1, 2 * TM), F32), pltpu.VMEM((128, 2 * TM), F32),
        ],
        compiler_params=_cparams(3),
        name="attn_" + mode,
    )(lam, g, qt, k, vt)


def _post_body(*refs, has_gate):
    if has_gate:
        x_ref, o_ref, gt_ref, mod_ref, wo_ref, n2_ref, w1_ref, w3_ref, w2_ref, out_ref = refs
        o = (o_ref[0] * gt_ref[0]).astype(BF16)
    else:
        x_ref, o_ref, mod_ref, wo_ref, n2_ref, w1_ref, w3_ref, w2_ref, out_ref = refs
        o = o_ref[0]
    m = mod_ref[0]
    x1 = x_ref[0] + m[2:3] * _dot(o, wo_ref[...])
    h = _norm_mod(x1, n2_ref[...], m[3:4], m[4:5]).astype(BF16)
    a = _dot(h, w1_ref[...])
    b = _dot(h, w3_ref[...])
    u = (a * jax.nn.sigmoid(a) * b).astype(BF16)
    out_ref[0] = x1 + m[5:6] * _dot(u, w2_ref[...])


def _post_call(xa, o, gate, mod_i, wo, n2, w1, w3, w2, nct):
    B, T, D = xa.shape
    F = w1.shape[1]
    x_spec, mod_spec = _tok_specs(B, nct, D)
    has_gate = gate is not None
    ins = [xa, o] + ([gate] if has_gate else []) + [mod_i, wo, n2, w1, w3, w2]
    one = pl.Buffered(1)
    specs = [x_spec, x_spec] + ([x_spec] if has_gate else []) + [
        mod_spec,
        pl.BlockSpec((D, D), lambda b, t: (0, 0), pipeline_mode=one),
        _const_spec((1, D)),
        pl.BlockSpec((D, F), lambda b, t: (0, 0), pipeline_mode=one),
        pl.BlockSpec((D, F), lambda b, t: (0, 0), pipeline_mode=one),
        pl.BlockSpec((F, D), lambda b, t: (0, 0), pipeline_mode=one),
    ]
    return pl.pallas_call(
        functools.partial(_post_body, has_gate=has_gate),
        grid=(B, T // TM),
        in_specs=specs,
        out_specs=x_spec,
        out_shape=jax.ShapeDtypeStruct((B, T, D), F32),
        input_output_aliases={0: 0},
        compiler_params=_cparams(2),
        name="post_ffn",
    )(*ins)


def _mla_pre_body(x_ref, mod_ref, g_ref, wext_ref, gq_ref, gkv_ref, wuqt_ref, wkn_ref, wvt_ref,
                  cq_ref, sq_ref, ck_ref, sk_ref, qt_ref, k_ref, vt_ref, *, scale):
    m = mod_ref[0]
    hb = _norm_mod(x_ref[0], g_ref[...], m[0:1], m[1:2]).astype(BF16)
    down = _dot(hb, wext_ref[...])
    ql = down[:, :MLA_Q_LORA]
    cq_lat = (ql * lax.rsqrt(jnp.mean(ql * ql, axis=-1, keepdims=True) + NORM_EPS) * gq_ref[...]).astype(BF16)
    kl = down[:, MLA_Q_LORA:MLA_Q_LORA + MLA_KV_LORA]
    ckv = (kl * lax.rsqrt(jnp.mean(kl * kl, axis=-1, keepdims=True) + NORM_EPS) * gkv_ref[...]).astype(BF16)
    kr = (down[:, 384:512] * ck_ref[...] + down[:, 512:640] * sk_ref[...]).astype(BF16)
    kn = _dot(ckv, wkn_ref[...])
    npair = kn.shape[1] // 128
    for p in range(npair):
        k_ref[0, :, 256 * p:256 * p + 128] = kn[:, 128 * p:128 * (p + 1)].astype(BF16)
        k_ref[0, :, 256 * p + 128:256 * (p + 1)] = kr
    vt_ref[0, 0] = _dot_nt(wvt_ref[...], ckv).astype(BF16)
    qt = _dot_nt(wuqt_ref[...], cq_lat)
    cq = cq_ref[...]
    sq = sq_ref[...]
    for p in range(npair):
        base = 256 * p
        qt_ref[0, 0, base:base + 128, :] = (qt[base:base + 128] * scale).astype(BF16)
        blk = [qt[base + 128 + 8 * i:base + 136 + 8 * i] for i in range(8)]
        for i in range(8):
            out = blk[i] * cq[8 * i:8 * i + 8] + blk[i ^ 1] * sq[8 * i:8 * i + 8]
            qt_ref[0, 0, base + 128 + 8 * i:base + 136 + 8 * i, :] = out.astype(BF16)
        qt_ref[0, 0, base + 192:base + 256, :] = jnp.zeros((64, qt.shape[1]), BF16)


def _mla_pre_call(xa, mod_i, g1, wext, gq, gkv, wuqt, wkn, wvt, cq, sq, ck, sk, nct, scale):
    B, T, D = xa.shape
    NT = T // TM
    x_spec, mod_spec = _tok_specs(B, nct, D)
    GD = wuqt.shape[0]
    return pl.pallas_call(
        functools.partial(_mla_pre_body, scale=scale),
        grid=(B, NT),
        in_specs=[
            x_spec, mod_spec, _const_spec((1, D)),
            _const_spec(wext.shape), _const_spec(gq.shape), _const_spec(gkv.shape),
            _const_spec(wuqt.shape), _const_spec(wkn.shape), _const_spec(wvt.shape),
            pl.BlockSpec((64, TM), lambda b, t: (0, t)), pl.BlockSpec((64, TM), lambda b, t: (0, t)),
            pl.BlockSpec((TM, 128), lambda b, t: (t, 0)), pl.BlockSpec((TM, 128), lambda b, t: (t, 0)),
        ],
        out_specs=[
            pl.BlockSpec((1, 1, GD, TM), lambda b, t: (b, t, 0, 0)),
            pl.BlockSpec((1, TM, GD), lambda b, t: (b, t, 0)),
            pl.BlockSpec((1, 1, D, TM), lambda b, t: (b, t, 0, 0)),
        ],
        out_shape=[
            jax.ShapeDtypeStruct((B, NT, GD, TM), BF16),
            jax.ShapeDtypeStruct((B, T, GD), BF16),
            jax.ShapeDtypeStruct((B, NT, D, TM), BF16),
        ],
        compiler_params=_cparams(2),
        name="mla_pre",
    )(xa, mod_i, g1, wext, gq, gkv, wuqt, wkn, wvt, cq, sq, ck, sk)


def _rw_pre_body(x_ref, xp_ref, xn_ref, mod_ref, g_ref, mu_ref, wr_ref, wk_ref, wv_ref, w1_ref, w2_ref, w0_ref,
                 a1_ref, a2_ref, a0_ref, g1_ref, g2_ref, r_ref, k_ref, v_ref, w_ref, a_ref, gt_ref, *, nct, nt):
    t = pl.program_id(1)
    m = mod_ref[0]
    g = g_ref[...]
    h = _norm_mod(x_ref[0], g, m[0:1], m[1:2])
    hp = _norm_mod(xp_ref[0], g, m[0:1], m[1:2])[7:8]
    hn = _norm_mod(xn_ref[0], g, m[0:1], m[1:2])[0:1]
    has_prev = jnp.logical_and(t != 0, t != nct)
    has_next = jnp.logical_and(t != nct - 1, t != nt - 1)
    hp = jnp.where(has_prev, hp, 0.0)
    hn = jnp.where(has_next, hn, 0.0)
    tm = h.shape[0]
    rid = lax.broadcasted_iota(jnp.int32, h.shape, 0)
    h_m1 = jnp.where(rid == 0, hp, pltpu.roll(h, 1, axis=0))
    h_p1 = jnp.where(rid == tm - 1, hn, pltpu.roll(h, tm - 1, axis=0))
    cs = 0.5 * (h_m1 + h_p1) - h
    mu = mu_ref[...]

    def mix(s):
        return (h + cs * mu[s:s + 1]).astype(BF16)

    r_ref[0] = _dot(mix(0), wr_ref[...])
    k_ref[0] = _dot(mix(1), wk_ref[...])
    v_ref[0] = _dot(mix(2), wv_ref[...])
    lw = jnp.tanh(_dot(mix(3), w1_ref[...])).astype(BF16)
    la = _dot(mix(4), a1_ref[...]).astype(BF16)
    for z in range(2):
        u = -(w0_ref[z:z + 1] + _dot(lw, w2_ref[z]))
        sp = jnp.maximum(u, 0.0) + jnp.log(1.0 + jnp.exp(-jnp.abs(u)))
        w_ref[z, 0] = jnp.exp(-jnp.exp(-sp - 0.5))
        a_ref[z, 0] = jax.nn.sigmoid(a0_ref[z:z + 1] + _dot(la, a2_ref[z]))
    gl = jax.nn.sigmoid(_dot(mix(5), g1_ref[...])).astype(BF16)
    gt_ref[0] = _dot(gl, g2_ref[...])


def _rw_pre_call(xa, mod_i, g1n, mu, wr, wk, wv, w1, w2, w0, a1, a2, a0, g1, g2, nct):
    B, T, D = xa.shape
    NT = T // TM
    x_spec, mod_spec = _tok_specs(B, nct, D)
    r8 = TM // 8
    prev_spec = pl.BlockSpec((1, 8, D), lambda b, t: (b, jnp.maximum(t * r8 - 1, 0), 0))
    next_spec = pl.BlockSpec((1, 8, D), lambda b, t: (b, jnp.minimum((t + 1) * r8, T // 8 - 1), 0))
    dir_spec = pl.BlockSpec((2, 1, TM, D), lambda b, t: (0, b, t, 0))
    consts = [g1n, mu, wr, wk, wv, w1, w2, w0, a1, a2, a0, g1, g2]
    tok = jax.ShapeDtypeStruct((B, T, D), F32)
    two = jax.ShapeDtypeStruct((2, B, T, D), F32)
    return pl.pallas_call(
        functools.partial(_rw_pre_body, nct=nct, nt=NT),
        grid=(B, NT),
        in_specs=[x_spec, prev_spec, next_spec, mod_spec] + [_const_spec(c.shape) for c in consts],
        out_specs=[x_spec, x_spec, x_spec, dir_spec, dir_spec, x_spec],
        out_shape=[tok, tok, tok, two, two, tok],
        compiler_params=_cparams(2),
        name="rw_pre",
    )(xa, xa, xa, mod_i, *consts)


def _rw_scan_body(r_ref, k_ref, v_ref, w_ref, a_ref, kk_ref, ka_ref, y_ref, s_ref, *, tc):
    d = pl.program_id(0)

    @pl.when(pl.program_id(1) == 0)
    def _():
        s_ref[...] = jnp.zeros(s_ref.shape, F32)

    nv = s_ref.shape[0]

    def step(i, carry):
        te = jnp.where(d == 0, i, tc - 1 - i)
        r = r_ref[te]
        k = k_ref[te]
        w = w_ref[0, te]
        a = a_ref[0, te]
        kk = k * kk_ref[...]
        nrm = jnp.sqrt(jnp.sum(kk * kk, axis=0, keepdims=True))
        kk = kk / jnp.maximum(nrm, 1e-12)
        av = -kk
        bv = kk * a
        kd = k * (1.0 + (a - 1.0) * ka_ref[...])
        wr = w * r
        br = jnp.sum(bv * r, axis=0, keepdims=True)
        kr = jnp.sum(kd * r, axis=0, keepdims=True)
        for vi in range(nv):
            s = s_ref[vi]
            sa = jnp.sum(s * av, axis=0, keepdims=True)
            sq = jnp.sum(s * wr, axis=0, keepdims=True)
            vv = v_ref[te, pl.ds(vi, 1), :]
            y_ref[0, te, pl.ds(vi, 1), :] = sq + sa * br + vv * kr
            s_ref[vi] = s * w + sa * bv + vv * kd
        return carry

    lax.fori_loop(0, tc, step, 0)


def _rw_scan_call(r, k, v, w, a, kk_tab, ka_tab, ncc):
    T, K, BH = r.shape
    tc = SCAN_TC
    NC = T // tc

    def cidx(d, c):
        bwd = jnp.where(c < ncc, ncc - 1 - c, NC - 1 - (c - ncc))
        return jnp.where(d == 0, c, bwd)

    sh_spec = pl.BlockSpec((tc, K, BH), lambda d, c: (cidx(d, c), 0, 0))
    dir_spec = pl.BlockSpec((1, tc, K, BH), lambda d, c: (d, cidx(d, c), 0, 0))
    return pl.pallas_call(
        functools.partial(_rw_scan_body, tc=tc),
        grid=(2, NC),
        in_specs=[sh_spec, sh_spec, sh_spec, dir_spec, dir_spec, _const_spec((K, BH)), _const_spec((K, BH))],
        out_specs=dir_spec,
        out_shape=jax.ShapeDtypeStruct((2, T, K, BH), F32),
        scratch_shapes=[pltpu.VMEM((K, K, BH), F32)],
        compiler_params=_cparams(2),
        name="rw_scan",
    )(r, k, v, w, a, kk_tab, ka_tab)


def _rw_fin_body(y_ref, r_ref, k_ref, v_ref, a_ref, ka_ref, rk_ref, lg_ref, lb_ref, z_ref):
    y = y_ref[0] + y_ref[1]
    mean = jnp.mean(y, axis=1, keepdims=True)
    yc = y - mean
    var = jnp.mean(yc * yc, axis=1, keepdims=True)
    yn = yc * lax.rsqrt(var + RW_GN_EPS) * lg_ref[...] + lb_ref[...]
    k = k_ref[...]
    ka = ka_ref[...]
    ksum = k * (1.0 + (a_ref[0] - 1.0) * ka) + k * (1.0 + (a_ref[1] - 1.0) * ka)
    bonus = jnp.sum(r_ref[...] * ksum * rk_ref[...], axis=1, keepdims=True) * v_ref[...]
    z_ref[...] = yn + bonus


def _rw_fin_call(y, r, k, v, a, ka_tab, rk_tab, lg_tab, lb_tab):
    T, K, BH = r.shape
    tc = SCAN_TC
    sh_spec = pl.BlockSpec((tc, K, BH), lambda c: (c, 0, 0))
    dir_spec = pl.BlockSpec((2, tc, K, BH), lambda c: (0, c, 0, 0))
    tab = _const_spec((K, BH))
    return pl.pallas_call(
        _rw_fin_body,
        grid=(T // tc,),
        in_specs=[dir_spec, sh_spec, sh_spec, sh_spec, dir_spec, tab, tab, tab, tab],
        out_specs=sh_spec,
        out_shape=jax.ShapeDtypeStruct((T, K, BH), F32),
        compiler_params=_cparams(1),
        name="rw_fin",
    )(y, r, k, v, a, ka_tab, rk_tab, lg_tab, lb_tab)


def _final_body(x_ref, g_ref, o_ref):
    x = x_ref[0]
    ms = jnp.mean(x * x, axis=-1, keepdims=True)
    o_ref[0] = x * lax.rsqrt(ms + NORM_EPS) * g_ref[...]


def _final_call(xa, g, nct):
    B, T, D = xa.shape
    N = T - nct * TM
    return pl.pallas_call(
        _final_body,
        grid=(B, N // TM),
        in_specs=[pl.BlockSpec((1, TM, D), lambda b, t: (b, t + nct, 0)), _const_spec((1, D))],
        out_specs=pl.BlockSpec((1, TM, D), lambda b, t: (b, t, 0)),
        out_shape=jax.ShapeDtypeStruct((B, N, D), F32),
        compiler_params=_cparams(2),
        name="final_norm",
    )(xa, g)


def _rope_tables(n_ctx, n_lat, rot_dim):
    mfreq = rot_dim // 4
    t = jnp.arange(n_lat)
    row = (t // GRID_W).astype(F32)
    col = (t % GRID_W).astype(F32)
    inv = ROPE_BASE ** (-jnp.arange(mfreq, dtype=F32) / mfreq)
    ar = row[:, None] * inv
    ac = col[:, None] * inv
    cos = jnp.concatenate([jnp.cos(ar), jnp.cos(ar), jnp.cos(ac), jnp.cos(ac)], axis=1)
    sin = jnp.concatenate([-jnp.sin(ar), jnp.sin(ar), -jnp.sin(ac), jnp.sin(ac)], axis=1)
    cos = jnp.concatenate([jnp.ones((n_ctx, rot_dim), F32), cos], axis=0)
    sin = jnp.concatenate([jnp.zeros((n_ctx, rot_dim), F32), sin], axis=0)
    return cos, sin


def _swap_perm(n, half):
    idx = jnp.arange(n)
    return idx ^ half


def _to_scan(x, B):
    lead = x.shape[:-3]
    T, D = x.shape[-2:]
    H = D // RW_HEAD
    x = x.reshape(lead + (B, T, H, RW_HEAD))
    nl = len(lead)
    perm = tuple(range(nl)) + (nl + 1, nl + 3, nl, nl + 2)
    return jnp.transpose(x, perm).reshape(lead + (T, RW_HEAD, B * H))


def _head_tab(p, B):
    H = p.size // RW_HEAD
    t = p.reshape(H, RW_HEAD).T
    return jnp.tile(t[:, None, :], (1, B, 1)).reshape(RW_HEAD, B * H).astype(F32)


def kernel(x, c, ctx, c_ctx, ada_w, ada_b, norm1_g, norm2_g, ffn_w1, ffn_w3, ffn_w2, final_norm_g, da_wqkv, da_lambda, da_subln_g, da_wo, rw_mu, rw_wrkv, rw_w0, rw_w1, rw_w2, rw_a0, rw_a1, rw_a2, rw_g1, rw_g2, rw_k_k, rw_k_a, rw_r_k, rw_lnx_g, rw_lnx_b, rw_wo, mla_wdown, mla_q_norm_g, mla_wuq, mla_kv_norm_g, mla_wukv, mla_wo):
    B, N, D = x.shape
    n_ctx = ctx.shape[1]
    depth = ada_w.shape[0]
    assert n_ctx % TM == 0 and N % TM == 0 and n_ctx % SCAN_TC == 0
    nct = n_ctx // TM
    T = n_ctx + N
    assert (T // TM) % KV_TILES == 0

    xa = jnp.concatenate([ctx, x], axis=1)
    R = -(-(B + 1) // 8) * 8
    cond = jnp.zeros((R, D), F32).at[:B].set(c).at[B].set(c_ctx)
    mod = _ada_call(cond, ada_w, ada_b).reshape(depth, R, 6, D)

    da_cos, da_sin = _rope_tables(n_ctx, N, DA_HEAD_DIM)
    da_scale = (DA_HEAD_DIM ** -0.5) * LOG2E
    da_cq, da_sq = (da_cos * da_scale).T, (da_sin * da_scale).T
    da_ck, da_sk = jnp.tile(da_cos, (1, 2)), jnp.tile(da_sin, (1, 2))
    ml_cos, ml_sin = _rope_tables(n_ctx, N, MLA_ROPE)
    ml_scale = ((MLA_NOPE + MLA_ROPE) ** -0.5) * LOG2E
    ml_cq, ml_sq = jnp.tile((ml_cos * ml_scale).T, (2, 1)), jnp.tile((ml_sin * ml_scale).T, (2, 1))
    zpad = jnp.zeros((T, 64), F32)
    ml_ck = jnp.concatenate([ml_cos, ml_cos, zpad], axis=1)
    ml_sk = jnp.concatenate([ml_sin, ml_sin, zpad], axis=1)

    for i in range(depth):
        kind, j = i % N_MIXERS, i // N_MIXERS
        mod_i = mod[i]
        g1 = norm1_g[i].reshape(1, D)
        gate = None
        if kind == 0:
            wq, wk, wv = jnp.split(da_wqkv[j], 3, axis=1)
            perm = _swap_perm(D, 16)
            qt, kk, vt = _da_pre_call(xa, mod_i, g1, wq.T.astype(BF16), wk.astype(BF16), wk[:, perm].astype(BF16),
                                      wv.T.astype(BF16), da_cq, da_sq, da_ck, da_sk, nct)
            lam_init = 0.8 - 0.6 * math.exp(-0.3 * i)
            o = _attn_call(da_lambda[j], da_subln_g[j].reshape(1, -1), qt, kk, vt,
                           mode="diff", nct=nct, lam_init=lam_init)
            wo = da_wo[j]
        elif kind == 1:
            H = D // RW_HEAD
            w1c = jnp.concatenate([rw_w1[j, 0], rw_w1[j, 1]], axis=1).astype(BF16)
            a1c = jnp.concatenate([rw_a1[j, 0], rw_a1[j, 1]], axis=1).astype(BF16)
            zl = jnp.zeros_like(rw_w2[j, 0])
            w2p = jnp.stack([jnp.concatenate([rw_w2[j, 0], zl], axis=0),
                             jnp.concatenate([zl, rw_w2[j, 1]], axis=0)]).astype(BF16)
            a2p = jnp.stack([jnp.concatenate([rw_a2[j, 0], zl], axis=0),
                             jnp.concatenate([zl, rw_a2[j, 1]], axis=0)]).astype(BF16)
            gl = rw_g1.shape[2]
            glp = -(-gl // 128) * 128
            g1p = jnp.pad(rw_g1[j], ((0, 0), (0, glp - gl))).astype(BF16)
            g2p = jnp.pad(rw_g2[j], ((0, glp - gl), (0, 0))).astype(BF16)
            r, k, v, w, a, gate = _rw_pre_call(
                xa, mod_i, g1, rw_mu[j], rw_wrkv[j, 0].astype(BF16), rw_wrkv[j, 1].astype(BF16),
                rw_wrkv[j, 2].astype(BF16), w1c, w2p, rw_w0[j], a1c, a2p, rw_a0[j], g1p, g2p, nct)
            rs, ks, vs, ws, as_ = (_to_scan(u, B) for u in (r, k, v, w, a))
            ka_tab = _head_tab(rw_k_a[j], B)
            y = _rw_scan_call(rs, ks, vs, ws, as_, _head_tab(rw_k_k[j], B), ka_tab, n_ctx // SCAN_TC)
            z = _rw_fin_call(y, rs, ks, vs, as_, ka_tab, _head_tab(rw_r_k[j], B),
                             _head_tab(rw_lnx_g[j], B), _head_tab(rw_lnx_b[j], B))
            o = jnp.transpose(z.reshape(T, RW_HEAD, B, H), (2, 0, 3, 1)).reshape(B, T, D)
            wo = rw_wo[j]
        else:
            wd = mla_wdown[j]
            nlat = MLA_Q_LORA + MLA_KV_LORA
            kr = wd[:, nlat:]
            krs = kr[:, _swap_perm(MLA_ROPE, 8)]
            zc = jnp.zeros((D, 64), F32)
            wext = jnp.concatenate([wd[:, :nlat], kr, kr, zc, krs, krs, zc], axis=1).astype(BF16)
            wuq = mla_wuq[j].reshape(MLA_Q_LORA, MLA_HEADS // 2, 2, MLA_NOPE + MLA_ROPE)
            qn = wuq[..., :MLA_NOPE].reshape(MLA_Q_LORA, MLA_HEADS // 2, 2 * MLA_NOPE)
            qr = wuq[..., MLA_NOPE:].reshape(MLA_Q_LORA, MLA_HEADS // 2, 2 * MLA_ROPE)
            qz = jnp.zeros((MLA_Q_LORA, MLA_HEADS // 2, 64), F32)
            wuqt = jnp.concatenate([qn, qr, qz], axis=2).reshape(MLA_Q_LORA, -1).T.astype(BF16)
            wukv = mla_wukv[j].reshape(MLA_KV_LORA, MLA_HEADS, MLA_NOPE + MLA_V)
            wkn = wukv[..., :MLA_NOPE].reshape(MLA_KV_LORA, -1).astype(BF16)
            wvt = wukv[..., MLA_NOPE:].reshape(MLA_KV_LORA, -1).T.astype(BF16)
            qt, kk, vt = _mla_pre_call(xa, mod_i, g1, wext, mla_q_norm_g[j].reshape(1, -1),
                                       mla_kv_norm_g[j].reshape(1, -1), wuqt, wkn, wvt,
                                       ml_cq, ml_sq, ml_ck, ml_sk, nct, ml_scale)
            o = _attn_call(jnp.zeros((4, 64), F32), jnp.ones((1, 128), F32), qt, kk, vt,
                           mode="mla", nct=nct, lam_init=0.0)
            wo = mla_wo[j]
        xa = _post_call(xa, o, gate, mod_i, wo.astype(BF16), norm2_g[i].reshape(1, D),
                        ffn_w1[i].astype(BF16), ffn_w3[i].astype(BF16), ffn_w2[i].astype(BF16), nct)
    return _final_call(xa, final_norm_g.reshape(1, D), nct)
```

```python
import functools
import math

import jax
import jax.numpy as jnp
from jax import lax
from jax.experimental import pallas as pl
from jax.experimental.pallas import tpu as pltpu

F32 = jnp.float32
BF16 = jnp.bfloat16

GRID_W = 64
ROPE_BASE = 10000.0
NORM_EPS = 1e-6
N_MIXERS = 3
DA_HEAD_DIM = 64
DA_SUBLN_EPS = 1e-5
RW_HEAD = 64
RW_GN_EPS = 64e-5
MLA_HEADS = 16
MLA_Q_LORA = 256
MLA_KV_LORA = 128
MLA_NOPE = 64
MLA_ROPE = 32
MLA_V = 64
LOG2E = math.log2(math.e)

TM = 256
KV_TILES = 3
SCAN_TC = 32
ADA_TN = 1536
VMEM_LIMIT = 56 * 1024 * 1024


def _cparams(n_grid):
    return pltpu.CompilerParams(dimension_semantics=("arbitrary",) * n_grid, vmem_limit_bytes=VMEM_LIMIT)


def _const_spec(shape):
    nd = len(shape)
    return pl.BlockSpec(shape, lambda *_: (0,) * nd)


def _norm_mod(x, g, shift, scale):
    ms = jnp.mean(x * x, axis=-1, keepdims=True)
    return (x * lax.rsqrt(ms + NORM_EPS) * g) * (1.0 + scale) + shift


def _dot(a, b):
    return jnp.dot(a, b, preferred_element_type=F32)


def _dot_nt(a, b):
    return lax.dot_general(a, b, (((1,), (1,)), ((), ())), preferred_element_type=F32)


def _ada_body(c_ref, w_ref, b_ref, o_ref):
    c = c_ref[...]
    s = (c * jax.nn.sigmoid(c)).astype(BF16)
    o_ref[0] = _dot(s, w_ref[0].astype(BF16)) + b_ref[0]


def _ada_call(cond, ada_w, ada_b):
    L, D, D6 = ada_w.shape
    R = cond.shape[0]
    return pl.pallas_call(
        _ada_body,
        grid=(L, D6 // ADA_TN),
        in_specs=[
            pl.BlockSpec((R, D), lambda l, j: (0, 0)),
            pl.BlockSpec((1, D, ADA_TN), lambda l, j: (l, 0, j)),
            pl.BlockSpec((1, 1, ADA_TN), lambda l, j: (l, 0, j)),
        ],
        out_specs=pl.BlockSpec((1, R, ADA_TN), lambda l, j: (l, 0, j)),
        out_shape=jax.ShapeDtypeStruct((L, R, D6), F32),
        compiler_params=_cparams(2),
        name="ada_mod",
    )(cond, ada_w, ada_b.reshape(L, 1, D6))


def _tok_specs(B, nct, D):
    x_spec = pl.BlockSpec((1, TM, D), lambda b, t: (b, t, 0))
    mod_spec = pl.BlockSpec((1, 6, D), lambda b, t: (jnp.where(t < nct, B, b), 0, 0))
    return x_spec, mod_spec


def _da_pre_body(x_ref, mod_ref, g_ref, wqt_ref, wk_ref, wks_ref, wvt_ref, cq_ref, sq_ref, ck_ref, sk_ref,
                 qt_ref, k_ref, vt_ref):
    m = mod_ref[0]
    hb = _norm_mod(x_ref[0], g_ref[...], m[0:1], m[1:2]).astype(BF16)
    D = hb.shape[1]
    qt = _dot_nt(wqt_ref[...], hb)
    cq = cq_ref[...]
    sq = sq_ref[...]
    hd = DA_HEAD_DIM
    for h in range(D // hd):
        blk = [qt[h * hd + 16 * i:h * hd + 16 * (i + 1)] for i in range(4)]
        for i in range(4):
            r0 = 16 * i
            out = blk[i] * cq[r0:r0 + 16] + blk[i ^ 1] * sq[r0:r0 + 16]
            qt_ref[0, 0, h * hd + r0:h * hd + r0 + 16, :] = out.astype(BF16)
    k = _dot(hb, wk_ref[...])
    ks = _dot(hb, wks_ref[...])
    ck = ck_ref[...]
    sk = sk_ref[...]
    for j in range(D // 128):
        sl = slice(128 * j, 128 * (j + 1))
        k_ref[0, :, sl] = (k[:, sl] * ck + ks[:, sl] * sk).astype(BF16)
    vt_ref[0, 0] = _dot_nt(wvt_ref[...], hb).astype(BF16)


def _da_pre_call(xa, mod_i, g1, wqt, wk, wks, wvt, cq, sq, ck, sk, nct):
    B, T, D = xa.shape
    NT = T // TM
    x_spec, mod_spec = _tok_specs(B, nct, D)
    fm_spec = pl.BlockSpec((1, 1, D, TM), lambda b, t: (b, t, 0, 0))
    return pl.pallas_call(
        _da_pre_body,
        grid=(B, NT),
        in_specs=[
            x_spec, mod_spec, _const_spec((1, D)),
            _const_spec((D, D)), _const_spec((D, D)), _const_spec((D, D)), _const_spec((D, D)),
            pl.BlockSpec((64, TM), lambda b, t: (0, t)), pl.BlockSpec((64, TM), lambda b, t: (0, t)),
            pl.BlockSpec((TM, 128), lambda b, t: (t, 0)), pl.BlockSpec((TM, 128), lambda b, t: (t, 0)),
        ],
        out_specs=[fm_spec, x_spec, fm_spec],
        out_shape=[
            jax.ShapeDtypeStruct((B, NT, D, TM), BF16),
            jax.ShapeDtypeStruct((B, T, D), BF16),
            jax.ShapeDtypeStruct((B, NT, D, TM), BF16),
        ],
        compiler_params=_cparams(2),
        name="da_pre",
    )(xa, mod_i, g1, wqt, wk, wks, wvt, cq, sq, ck, sk)


def _attn_body(lam_ref, g_ref, q_ref, k_ref, v_ref, o_ref, m_ref, l_ref, acc_ref, s_ref, mx_ref, q2_ref,
               *, mode, nct, nt, lam_init):
    dk, tq = q_ref.shape[2], q_ref.shape[3]

    def prep_q(qb):
        q = q_ref[0, qb].astype(F32)
        rows = lax.broadcasted_iota(jnp.int32, (dk, tq), 0)
        if mode == "diff":
            sel_a = rows < 64
            qa = jnp.where(sel_a, q, 0.0)
            qb_ = jnp.where(sel_a, 0.0, q)
        else:
            blk = rows // 32
            in_a = (blk == 0) | (blk == 1) | (blk == 4)
            in_b = (blk == 2) | (blk == 3) | (blk == 5)
            qa = jnp.where(in_a, q, 0.0)
            qb_ = jnp.where(in_b, q, 0.0)
        q2_ref[...] = jnp.concatenate([qa, qb_], axis=1).astype(BF16)

    def init():
        m_ref[...] = jnp.full(m_ref.shape, -1e30, F32)
        l_ref[...] = jnp.zeros(l_ref.shape, F32)
        acc_ref[...] = jnp.zeros(acc_ref.shape, F32)

    def qk(c0, ntile, slot):
        tk = ntile * TM
        k0 = c0 * TM if isinstance(c0, int) else pl.multiple_of(c0 * TM, TM)
        s = _dot(k_ref[0, pl.ds(k0, tk), :], q2_ref[...])
        s_ref[slot, 0:tk, :] = s
        mx_ref[slot] = jnp.max(s, axis=0, keepdims=True)

    def softmax_pv(c0, ntile, slot):
        s = s_ref[slot, 0:ntile * TM, :]
        m_prev = m_ref[...]
        m_new = jnp.maximum(m_prev, mx_ref[slot])
        alpha = jnp.exp2(m_prev - m_new)
        p = jnp.exp2(s - m_new)
        l_ref[...] = alpha * l_ref[...] + jnp.sum(p, axis=0, keepdims=True)
        pb = p.astype(BF16)
        pv = _dot(v_ref[0, c0], pb[0:TM])
        for j in range(1, ntile):
            pv = pv + _dot(v_ref[0, c0 + j], pb[j * TM:(j + 1) * TM])
        acc_ref[...] = alpha * acc_ref[...] + pv
        m_ref[...] = m_new

    def finish(qb):
        a = acc_ref[...] * (1.0 / l_ref[...])
        if mode == "diff":
            lv = lam_ref[...]
            lam = (jnp.exp(jnp.sum(lv[0:1] * lv[1:2], keepdims=True))
                   - jnp.exp(jnp.sum(lv[2:3] * lv[3:4], keepdims=True)) + lam_init)
            o = a[:, :tq] - lam * a[:, tq:]
            ms = jnp.mean(o * o, axis=0, keepdims=True)
            o = o * lax.rsqrt(ms + DA_SUBLN_EPS)
            ot = (o.T * g_ref[...]) * (1.0 - lam_init)
        else:
            ot = jnp.concatenate([a[:64, :tq], a[64:, tq:]], axis=0).T
        if isinstance(qb, int):
            o_ref[0, qb * TM:(qb + 1) * TM, :] = ot.astype(BF16)
        else:
            o_ref[0, pl.ds(pl.multiple_of(qb * TM, TM), TM), :] = ot.astype(BF16)

    for qb in range(nct):
        prep_q(qb)
        init()
        for c in range(nct):
            qk(c, 1, 0)
            softmax_pv(c, 1, 0)
        finish(qb)

    n = nt // KV_TILES
    nq = nt - nct
    last = (n - 1) * KV_TILES

    def tile_core(slot0):
        init()

        def pair(j, carry):
            c = 2 * j * KV_TILES
            qk(c + KV_TILES, KV_TILES, 1 - slot0)
            softmax_pv(c, KV_TILES, slot0)
            qk(c + 2 * KV_TILES, KV_TILES, slot0)
            softmax_pv(c + KV_TILES, KV_TILES, 1 - slot0)
            return carry

        lax.fori_loop(0, (n - 1) // 2, pair, 0)

    def tile_end(qb, slot0, qb_next):
        prep_q(qb_next)
        qk(0, KV_TILES, 1 - slot0)
        softmax_pv(last, KV_TILES, slot0)
        finish(qb)

    prep_q(nct)
    qk(0, KV_TILES, 0)

    def two_tiles(i, carry):
        qb = nct + 2 * i
        tile_core(0)
        tile_end(qb, 0, qb + 1)
        tile_core(1)
        tile_end(qb + 1, 1, jnp.minimum(qb + 2, nt - 1))
        return carry

    lax.fori_loop(0, nq // 2, two_tiles, 0)
    if nq % 2:
        tile_core(0)
        softmax_pv(last, KV_TILES, 0)
        finish(nt - 1)


def _attn_call(lam, g, qt, k, vt, *, mode, nct, lam_init):
    B, NT, GD, _ = qt.shape
    T = NT * TM
    G = vt.shape[2] // 128
    dk = GD // G
    body = functools.partial(_attn_body, mode=mode, nct=nct, nt=NT, lam_init=lam_init)
    return pl.pallas_call(
        body,
        grid=(B, G),
        in_specs=[
            _const_spec(lam.shape), _const_spec(g.shape),
            pl.BlockSpec((1, NT, dk, TM), lambda b, h: (b, 0, h, 0)),
            pl.BlockSpec((1, T, dk), lambda b, h: (b, 0, h)),
            pl.BlockSpec((1, NT, 128, TM), lambda b, h: (b, 0, h, 0)),
        ],
        out_specs=pl.BlockSpec((1, T, 128), lambda b, h: (b, 0, h)),
        out_shape=jax.ShapeDtypeStruct((B, T, G * 128), BF16),
        scratch_shapes=[
            pltpu.VMEM((1, 2 * TM), F32), pltpu.VMEM((1, 2 * TM), F32), pltpu.VMEM((128, 2 * TM), F32),
            pltpu.VMEM((2, KV_TILES * TM, 2 * TM), F32), pltpu.VMEM((2, 1, 2 * TM), F32),
            pltpu.VMEM((dk, 2 * TM), BF16),
        ],
        compiler_params=_cparams(2),
        name="attn_" + mode,
    )(lam, g, qt, k, vt)


def _post_body(*refs, has_gate):
    if has_gate:
        x_ref, o_ref, gt_ref, mod_ref, wo_ref, n2_ref, w1_ref, w3_ref, w2_ref, out_ref = refs
        o = (o_ref[0] * gt_ref[0]).astype(BF16)
    else:
        x_ref, o_ref, mod_ref, wo_ref, n2_ref, w1_ref, w3_ref, w2_ref, out_ref = refs
        o = o_ref[0]
    m = mod_ref[0]
    x1 = x_ref[0] + m[2:3] * _dot(o, wo_ref[...])
    h = _norm_mod(x1, n2_ref[...], m[3:4], m[4:5]).astype(BF16)
    a = _dot(h, w1_ref[...])
    b = _dot(h, w3_ref[...])
    u = (a * jax.nn.sigmoid(a) * b).astype(BF16)
    out_ref[0] = x1 + m[5:6] * _dot(u, w2_ref[...])


def _post_call(xa, o, gate, mod_i, wo, n2, w1, w3, w2, nct):
    B, T, D = xa.shape
    F = w1.shape[1]
    x_spec, mod_spec = _tok_specs(B, nct, D)
    has_gate = gate is not None
    ins = [xa, o] + ([gate] if has_gate else []) + [mod_i, wo, n2, w1, w3, w2]
    one = pl.Buffered(1)
    specs = [x_spec, x_spec] + ([x_spec] if has_gate else []) + [
        mod_spec,
        pl.BlockSpec((D, D), lambda b, t: (0, 0), pipeline_mode=one),
        _const_spec((1, D)),
        pl.BlockSpec((D, F), lambda b, t: (0, 0), pipeline_mode=one),
        pl.BlockSpec((D, F), lambda b, t: (0, 0), pipeline_mode=one),
        pl.BlockSpec((F, D), lambda b, t: (0, 0), pipeline_mode=one),
    ]
    return pl.pallas_call(
        functools.partial(_post_body, has_gate=has_gate),
        grid=(B, T // TM),
        in_specs=specs,
        out_specs=x_spec,
        out_shape=jax.ShapeDtypeStruct((B, T, D), F32),
        input_output_aliases={0: 0},
        compiler_params=_cparams(2),
        name="post_ffn",
    )(*ins)


def _mla_pre_body(x_ref, mod_ref, g_ref, wext_ref, gq_ref, gkv_ref, wuqt_ref, wkn_ref, wvt_ref,
                  cq_ref, sq_ref, ck_ref, sk_ref, qt_ref, k_ref, vt_ref, *, scale):
    m = mod_ref[0]
    hb = _norm_mod(x_ref[0], g_ref[...], m[0:1], m[1:2]).astype(BF16)
    down = _dot(hb, wext_ref[...])
    ql = down[:, :MLA_Q_LORA]
    cq_lat = (ql * lax.rsqrt(jnp.mean(ql * ql, axis=-1, keepdims=True) + NORM_EPS) * gq_ref[...]).astype(BF16)
    kl = down[:, MLA_Q_LORA:MLA_Q_LORA + MLA_KV_LORA]
    ckv = (kl * lax.rsqrt(jnp.mean(kl * kl, axis=-1, keepdims=True) + NORM_EPS) * gkv_ref[...]).astype(BF16)
    kr = (down[:, 384:512] * ck_ref[...] + down[:, 512:640] * sk_ref[...]).astype(BF16)
    kn = _dot(ckv, wkn_ref[...])
    npair = kn.shape[1] // 128
    for p in range(npair):
        k_ref[0, :, 256 * p:256 * p + 128] = kn[:, 128 * p:128 * (p + 1)].astype(BF16)
        k_ref[0, :, 256 * p + 128:256 * (p + 1)] = kr
    vt_ref[0, 0] = _dot_nt(wvt_ref[...], ckv).astype(BF16)
    qt = _dot_nt(wuqt_ref[...], cq_lat)
    cq = cq_ref[...]
    sq = sq_ref[...]
    for p in range(npair):
        base = 256 * p
        qt_ref[0, 0, base:base + 128, :] = (qt[base:base + 128] * scale).astype(BF16)
        blk = [qt[base + 128 + 8 * i:base + 136 + 8 * i] for i in range(8)]
        for i in range(8):
            out = blk[i] * cq[8 * i:8 * i + 8] + blk[i ^ 1] * sq[8 * i:8 * i + 8]
            qt_ref[0, 0, base + 128 + 8 * i:base + 136 + 8 * i, :] = out.astype(BF16)
        qt_ref[0, 0, base + 192:base + 256, :] = jnp.zeros((64, qt.shape[1]), BF16)


def _mla_pre_call(xa, mod_i, g1, wext, gq, gkv, wuqt, wkn, wvt, cq, sq, ck, sk, nct, scale):
    B, T, D = xa.shape
    NT = T // TM
    x_spec, mod_spec = _tok_specs(B, nct, D)
    GD = wuqt.shape[0]
    return pl.pallas_call(
        functools.partial(_mla_pre_body, scale=scale),
        grid=(B, NT),
        in_specs=[
            x_spec, mod_spec, _const_spec((1, D)),
            _const_spec(wext.shape), _const_spec(gq.shape), _const_spec(gkv.shape),
            _const_spec(wuqt.shape), _const_spec(wkn.shape), _const_spec(wvt.shape),
            pl.BlockSpec((64, TM), lambda b, t: (0, t)), pl.BlockSpec((64, TM), lambda b, t: (0, t)),
            pl.BlockSpec((TM, 128), lambda b, t: (t, 0)), pl.BlockSpec((TM, 128), lambda b, t: (t, 0)),
        ],
        out_specs=[
            pl.BlockSpec((1, 1, GD, TM), lambda b, t: (b, t, 0, 0)),
            pl.BlockSpec((1, TM, GD), lambda b, t: (b, t, 0)),
            pl.BlockSpec((1, 1, D, TM), lambda b, t: (b, t, 0, 0)),
        ],
        out_shape=[
            jax.ShapeDtypeStruct((B, NT, GD, TM), BF16),
            jax.ShapeDtypeStruct((B, T, GD), BF16),
            jax.ShapeDtypeStruct((B, NT, D, TM), BF16),
        ],
        compiler_params=_cparams(2),
        name="mla_pre",
    )(xa, mod_i, g1, wext, gq, gkv, wuqt, wkn, wvt, cq, sq, ck, sk)


def _rw_pre_body(x_ref, xp_ref, xn_ref, mod_ref, g_ref, mu_ref, wr_ref, wk_ref, wv_ref, w1_ref, w2_ref, w0_ref,
                 a1_ref, a2_ref, a0_ref, g1_ref, g2_ref, r_ref, k_ref, v_ref, w_ref, a_ref, gt_ref, *, nct, nt):
    t = pl.program_id(1)
    m = mod_ref[0]
    g = g_ref[...]
    h = _norm_mod(x_ref[0], g, m[0:1], m[1:2])
    hp = _norm_mod(xp_ref[0], g, m[0:1], m[1:2])[7:8]
    hn = _norm_mod(xn_ref[0], g, m[0:1], m[1:2])[0:1]
    has_prev = jnp.logical_and(t != 0, t != nct)
    has_next = jnp.logical_and(t != nct - 1, t != nt - 1)
    hp = jnp.where(has_prev, hp, 0.0)
    hn = jnp.where(has_next, hn, 0.0)
    tm = h.shape[0]
    rid = lax.broadcasted_iota(jnp.int32, h.shape, 0)
    h_m1 = jnp.where(rid == 0, hp, pltpu.roll(h, 1, axis=0))
    h_p1 = jnp.where(rid == tm - 1, hn, pltpu.roll(h, tm - 1, axis=0))
    cs = 0.5 * (h_m1 + h_p1) - h
    mu = mu_ref[...]

    def mix(s):
        return (h + cs * mu[s:s + 1]).astype(BF16)

    r_ref[0] = _dot(mix(0), wr_ref[...])
    k_ref[0] = _dot(mix(1), wk_ref[...])
    v_ref[0] = _dot(mix(2), wv_ref[...])
    lw = jnp.tanh(_dot(mix(3), w1_ref[...])).astype(BF16)
    la = _dot(mix(4), a1_ref[...]).astype(BF16)
    for z in range(2):
        u = -(w0_ref[z:z + 1] + _dot(lw, w2_ref[z]))
        sp = jnp.maximum(u, 0.0) + jnp.log(1.0 + jnp.exp(-jnp.abs(u)))
        w_ref[z, 0] = jnp.exp(-jnp.exp(-sp - 0.5))
        a_ref[z, 0] = jax.nn.sigmoid(a0_ref[z:z + 1] + _dot(la, a2_ref[z]))
    gl = jax.nn.sigmoid(_dot(mix(5), g1_ref[...])).astype(BF16)
    gt_ref[0] = _dot(gl, g2_ref[...])


def _rw_pre_call(xa, mod_i, g1n, mu, wr, wk, wv, w1, w2, w0, a1, a2, a0, g1, g2, nct):
    B, T, D = xa.shape
    NT = T // TM
    x_spec, mod_spec = _tok_specs(B, nct, D)
    r8 = TM // 8
    prev_spec = pl.BlockSpec((1, 8, D), lambda b, t: (b, jnp.maximum(t * r8 - 1, 0), 0))
    next_spec = pl.BlockSpec((1, 8, D), lambda b, t: (b, jnp.minimum((t + 1) * r8, T // 8 - 1), 0))
    dir_spec = pl.BlockSpec((2, 1, TM, D), lambda b, t: (0, b, t, 0))
    consts = [g1n, mu, wr, wk, wv, w1, w2, w0, a1, a2, a0, g1, g2]
    tok = jax.ShapeDtypeStruct((B, T, D), F32)
    two = jax.ShapeDtypeStruct((2, B, T, D), F32)
    return pl.pallas_call(
        functools.partial(_rw_pre_body, nct=nct, nt=NT),
        grid=(B, NT),
        in_specs=[x_spec, prev_spec, next_spec, mod_spec] + [_const_spec(c.shape) for c in consts],
        out_specs=[x_spec, x_spec, x_spec, dir_spec, dir_spec, x_spec],
        out_shape=[tok, tok, tok, two, two, tok],
        compiler_params=_cparams(2),
        name="rw_pre",
    )(xa, xa, xa, mod_i, *consts)


def _rw_scan_body(r_ref, k_ref, v_ref, w_ref, a_ref, kk_ref, ka_ref, y_ref, s_ref, *, tc):
    d = pl.program_id(0)

    @pl.when(pl.program_id(1) == 0)
    def _():
        s_ref[...] = jnp.zeros(s_ref.shape, F32)

    nv = s_ref.shape[0]

    def step(i, carry):
        te = jnp.where(d == 0, i, tc - 1 - i)
        r = r_ref[te]
        k = k_ref[te]
        w = w_ref[0, te]
        a = a_ref[0, te]
        kk = k * kk_ref[...]
        nrm = jnp.sqrt(jnp.sum(kk * kk, axis=0, keepdims=True))
        kk = kk / jnp.maximum(nrm, 1e-12)
        av = -kk
        bv = kk * a
        kd = k * (1.0 + (a - 1.0) * ka_ref[...])
        wr = w * r
        br = jnp.sum(bv * r, axis=0, keepdims=True)
        kr = jnp.sum(kd * r, axis=0, keepdims=True)
        for vi in range(nv):
            s = s_ref[vi]
            sa = jnp.sum(s * av, axis=0, keepdims=True)
            sq = jnp.sum(s * wr, axis=0, keepdims=True)
            vv = v_ref[te, pl.ds(vi, 1), :]
            y_ref[0, te, pl.ds(vi, 1), :] = sq + sa * br + vv * kr
            s_ref[vi] = s * w + sa * bv + vv * kd
        return carry

    lax.fori_loop(0, tc, step, 0)


def _rw_scan_call(r, k, v, w, a, kk_tab, ka_tab, ncc):
    T, K, BH = r.shape
    tc = SCAN_TC
    NC = T // tc

    def cidx(d, c):
        bwd = jnp.where(c < ncc, ncc - 1 - c, NC - 1 - (c - ncc))
        return jnp.where(d == 0, c, bwd)

    sh_spec = pl.BlockSpec((tc, K, BH), lambda d, c: (cidx(d, c), 0, 0))
    dir_spec = pl.BlockSpec((1, tc, K, BH), lambda d, c: (d, cidx(d, c), 0, 0))
    return pl.pallas_call(
        functools.partial(_rw_scan_body, tc=tc),
        grid=(2, NC),
        in_specs=[sh_spec, sh_spec, sh_spec, dir_spec, dir_spec, _const_spec((K, BH)), _const_spec((K, BH))],
        out_specs=dir_spec,
        out_shape=jax.ShapeDtypeStruct((2, T, K, BH), F32),
        scratch_shapes=[pltpu.VMEM((K, K, BH), F32)],
        compiler_params=_cparams(2),
        name="rw_scan",
    )(r, k, v, w, a, kk_tab, ka_tab)


def _rw_fin_body(y_ref, r_ref, k_ref, v_ref, a_ref, ka_ref, rk_ref, lg_ref, lb_ref, z_ref):
    y = y_ref[0] + y_ref[1]
    mean = jnp.mean(y, axis=1, keepdims=True)
    yc = y - mean
    var = jnp.mean(yc * yc, axis=1, keepdims=True)
    yn = yc * lax.rsqrt(var + RW_GN_EPS) * lg_ref[...] + lb_ref[...]
    k = k_ref[...]
    ka = ka_ref[...]
    ksum = k * (1.0 + (a_ref[0] - 1.0) * ka) + k * (1.0 + (a_ref[1] - 1.0) * ka)
    bonus = jnp.sum(r_ref[...] * ksum * rk_ref[...], axis=1, keepdims=True) * v_ref[...]
    z_ref[...] = yn + bonus


def _rw_fin_call(y, r, k, v, a, ka_tab, rk_tab, lg_tab, lb_tab):
    T, K, BH = r.shape
    tc = SCAN_TC
    sh_spec = pl.BlockSpec((tc, K, BH), lambda c: (c, 0, 0))
    dir_spec = pl.BlockSpec((2, tc, K, BH), lambda c: (0, c, 0, 0))
    tab = _const_spec((K, BH))
    return pl.pallas_call(
        _rw_fin_body,
        grid=(T // tc,),
        in_specs=[dir_spec, sh_spec, sh_spec, sh_spec, dir_spec, tab, tab, tab, tab],
        out_specs=sh_spec,
        out_shape=jax.ShapeDtypeStruct((T, K, BH), F32),
        compiler_params=_cparams(1),
        name="rw_fin",
    )(y, r, k, v, a, ka_tab, rk_tab, lg_tab, lb_tab)


def _final_body(x_ref, g_ref, o_ref):
    x = x_ref[0]
    ms = jnp.mean(x * x, axis=-1, keepdims=True)
    o_ref[0] = x * lax.rsqrt(ms + NORM_EPS) * g_ref[...]


def _final_call(xa, g, nct):
    B, T, D = xa.shape
    N = T - nct * TM
    return pl.pallas_call(
        _final_body,
        grid=(B, N // TM),
        in_specs=[pl.BlockSpec((1, TM, D), lambda b, t: (b, t + nct, 0)), _const_spec((1, D))],
        out_specs=pl.BlockSpec((1, TM, D), lambda b, t: (b, t, 0)),
        out_shape=jax.ShapeDtypeStruct((B, N, D), F32),
        compiler_params=_cparams(2),
        name="final_norm",
    )(xa, g)


def _rope_tables(n_ctx, n_lat, rot_dim):
    mfreq = rot_dim // 4
    t = jnp.arange(n_lat)
    row = (t // GRID_W).astype(F32)
    col = (t % GRID_W).astype(F32)
    inv = ROPE_BASE ** (-jnp.arange(mfreq, dtype=F32) / mfreq)
    ar = row[:, None] * inv
    ac = col[:, None] * inv
    cos = jnp.concatenate([jnp.cos(ar), jnp.cos(ar), jnp.cos(ac), jnp.cos(ac)], axis=1)
    sin = jnp.concatenate([-jnp.sin(ar), jnp.sin(ar), -jnp.sin(ac), jnp.sin(ac)], axis=1)
    cos = jnp.concatenate([jnp.ones((n_ctx, rot_dim), F32), cos], axis=0)
    sin = jnp.concatenate([jnp.zeros((n_ctx, rot_dim), F32), sin], axis=0)
    return cos, sin


def _swap_perm(n, half):
    idx = jnp.arange(n)
    return idx ^ half


def _to_scan(x, B):
    lead = x.shape[:-3]
    T, D = x.shape[-2:]
    H = D // RW_HEAD
    x = x.reshape(lead + (B, T, H, RW_HEAD))
    nl = len(lead)
    perm = tuple(range(nl)) + (nl + 1, nl + 3, nl, nl + 2)
    return jnp.transpose(x, perm).reshape(lead + (T, RW_HEAD, B * H))


def _head_tab(p, B):
    H = p.size // RW_HEAD
    t = p.reshape(H, RW_HEAD).T
    return jnp.tile(t[:, None, :], (1, B, 1)).reshape(RW_HEAD, B * H).astype(F32)


def kernel(x, c, ctx, c_ctx, ada_w, ada_b, norm1_g, norm2_g, ffn_w1, ffn_w3, ffn_w2, final_norm_g, da_wqkv, da_lambda, da_subln_g, da_wo, rw_mu, rw_wrkv, rw_w0, rw_w1, rw_w2, rw_a0, rw_a1, rw_a2, rw_g1, rw_g2, rw_k_k, rw_k_a, rw_r_k, rw_lnx_g, rw_lnx_b, rw_wo, mla_wdown, mla_q_norm_g, mla_wuq, mla_kv_norm_g, mla_wukv, mla_wo):
    B, N, D = x.shape
    n_ctx = ctx.shape[1]
    depth = ada_w.shape[0]
    assert n_ctx % TM == 0 and N % TM == 0 and n_ctx % SCAN_TC == 0
    nct = n_ctx // TM
    T = n_ctx + N
    assert (T // TM) % KV_TILES == 0 and (T // TM // KV_TILES) % 2 == 1

    xa = jnp.concatenate([ctx, x], axis=1)
    R = -(-(B + 1) // 8) * 8
    cond = jnp.zeros((R, D), F32).at[:B].set(c).at[B].set(c_ctx)
    mod = _ada_call(cond, ada_w, ada_b).reshape(depth, R, 6, D)

    da_cos, da_sin = _rope_tables(n_ctx, N, DA_HEAD_DIM)
    da_scale = (DA_HEAD_DIM ** -0.5) * LOG2E
    da_cq, da_sq = (da_cos * da_scale).T, (da_sin * da_scale).T
    da_ck, da_sk = jnp.tile(da_cos, (1, 2)), jnp.tile(da_sin, (1, 2))
    ml_cos, ml_sin = _rope_tables(n_ctx, N, MLA_ROPE)
    ml_scale = ((MLA_NOPE + MLA_ROPE) ** -0.5) * LOG2E
    ml_cq, ml_sq = jnp.tile((ml_cos * ml_scale).T, (2, 1)), jnp.tile((ml_sin * ml_scale).T, (2, 1))
    zpad = jnp.zeros((T, 64), F32)
    ml_ck = jnp.concatenate([ml_cos, ml_cos, zpad], axis=1)
    ml_sk = jnp.concatenate([ml_sin, ml_sin, zpad], axis=1)

    for i in range(depth):
        kind, j = i % N_MIXERS, i // N_MIXERS
        mod_i = mod[i]
        g1 = norm1_g[i].reshape(1, D)
        gate = None
        if kind == 0:
            wq, wk, wv = jnp.split(da_wqkv[j], 3, axis=1)
            perm = _swap_perm(D, 16)
            qt, kk, vt = _da_pre_call(xa, mod_i, g1, wq.T.astype(BF16), wk.astype(BF16), wk[:, perm].astype(BF16),
                                      wv.T.astype(BF16), da_cq, da_sq, da_ck, da_sk, nct)
            lam_init = 0.8 - 0.6 * math.exp(-0.3 * i)
            o = _attn_call(da_lambda[j], da_subln_g[j].reshape(1, -1), qt, kk, vt,
                           mode="diff", nct=nct, lam_init=lam_init)
            wo = da_wo[j]
        elif kind == 1:
            H = D // RW_HEAD
            w1c = jnp.concatenate([rw_w1[j, 0], rw_w1[j, 1]], axis=1).astype(BF16)
            a1c = jnp.concatenate([rw_a1[j, 0], rw_a1[j, 1]], axis=1).astype(BF16)
            zl = jnp.zeros_like(rw_w2[j, 0])
            w2p = jnp.stack([jnp.concatenate([rw_w2[j, 0], zl], axis=0),
                             jnp.concatenate([zl, rw_w2[j, 1]], axis=0)]).astype(BF16)
            a2p = jnp.stack([jnp.concatenate([rw_a2[j, 0], zl], axis=0),
                             jnp.concatenate([zl, rw_a2[j, 1]], axis=0)]).astype(BF16)
            gl = rw_g1.shape[2]
            glp = -(-gl // 128) * 128
            g1p = jnp.pad(rw_g1[j], ((0, 0), (0, glp - gl))).astype(BF16)
            g2p = jnp.pad(rw_g2[j], ((0, glp - gl), (0, 0))).astype(BF16)
            r, k, v, w, a, gate = _rw_pre_call(
                xa, mod_i, g1, rw_mu[j], rw_wrkv[j, 0].astype(BF16), rw_wrkv[j, 1].astype(BF16),
                rw_wrkv[j, 2].astype(BF16), w1c, w2p, rw_w0[j], a1c, a2p, rw_a0[j], g1p, g2p, nct)
            rs, ks, vs, ws, as_ = (_to_scan(u, B) for u in (r, k, v, w, a))
            ka_tab = _head_tab(rw_k_a[j], B)
            y = _rw_scan_call(rs, ks, vs, ws, as_, _head_tab(rw_k_k[j], B), ka_tab, n_ctx // SCAN_TC)
            z = _rw_fin_call(y, rs, ks, vs, as_, ka_tab, _head_tab(rw_r_k[j], B),
                             _head_tab(rw_lnx_g[j], B), _head_tab(rw_lnx_b[j], B))
            o = jnp.transpose(z.reshape(T, RW_HEAD, B, H), (2, 0, 3, 1)).reshape(B, T, D)
            wo = rw_wo[j]
        else:
            wd = mla_wdown[j]
            nlat = MLA_Q_LORA + MLA_KV_LORA
            kr = wd[:, nlat:]
            krs = kr[:, _swap_perm(MLA_ROPE, 8)]
            zc = jnp.zeros((D, 64), F32)
            wext = jnp.concatenate([wd[:, :nlat], kr, kr, zc, krs, krs, zc], axis=1).astype(BF16)
            wuq = mla_wuq[j].reshape(MLA_Q_LORA, MLA_HEADS // 2, 2, MLA_NOPE + MLA_ROPE)
            qn = wuq[..., :MLA_NOPE].reshape(MLA_Q_LORA, MLA_HEADS // 2, 2 * MLA_NOPE)
            qr = wuq[..., MLA_NOPE:].reshape(MLA_Q_LORA, MLA_HEADS // 2, 2 * MLA_ROPE)
            qz = jnp.zeros((MLA_Q_LORA, MLA_HEADS // 2, 64), F32)
            wuqt = jnp.concatenate([qn, qr, qz], axis=2).reshape(MLA_Q_LORA, -1).T.astype(BF16)
            wukv = mla_wukv[j].reshape(MLA_KV_LORA, MLA_HEADS, MLA_NOPE + MLA_V)
            wkn = wukv[..., :MLA_NOPE].reshape(MLA_KV_LORA, -1).astype(BF16)
            wvt = wukv[..., MLA_NOPE:].reshape(MLA_KV_LORA, -1).T.astype(BF16)
            qt, kk, vt = _mla_pre_call(xa, mod_i, g1, wext, mla_q_norm_g[j].reshape(1, -1),
                                       mla_kv_norm_g[j].reshape(1, -1), wuqt, wkn, wvt,
                                       ml_cq, ml_sq, ml_ck, ml_sk, nct, ml_scale)
            o = _attn_call(jnp.zeros((4, 64), F32), jnp.ones((1, 128), F32), qt, kk, vt,
                           mode="mla", nct=nct, lam_init=0.0)
            wo = mla_wo[j]
        xa = _post_call(xa, o, gate, mod_i, wo.astype(BF16), norm2_g[i].reshape(1, D),
                        ffn_w1[i].astype(BF16), ffn_w3[i].astype(BF16), ffn_w2[i].astype(BF16), nct)
    return _final_call(xa, final_norm_g.reshape(1, D), nct)
```

```python
import functools
import math

import jax
import jax.numpy as jnp
from jax import lax
from jax.experimental import pallas as pl
from jax.experimental.pallas import tpu as pltpu

F32 = jnp.float32
BF16 = jnp.bfloat16

GRID_W = 64
ROPE_BASE = 10000.0
NORM_EPS = 1e-6
N_MIXERS = 3
DA_HEAD_DIM = 64
DA_SUBLN_EPS = 1e-5
RW_HEAD = 64
RW_GN_EPS = 64e-5
MLA_HEADS = 16
MLA_Q_LORA = 256
MLA_KV_LORA = 128
MLA_NOPE = 64
MLA_ROPE = 32
MLA_V = 64
LOG2E = math.log2(math.e)

TM = 256
KV_TILES = 3
PAIR_UNROLL = 5
SCAN_TC = 32
ADA_TN = 1536
VMEM_LIMIT = 56 * 1024 * 1024


def _cparams(n_grid):
    return pltpu.CompilerParams(dimension_semantics=("arbitrary",) * n_grid, vmem_limit_bytes=VMEM_LIMIT)


def _const_spec(shape):
    nd = len(shape)
    return pl.BlockSpec(shape, lambda *_: (0,) * nd)


def _norm_mod(x, g, shift, scale):
    ms = jnp.mean(x * x, axis=-1, keepdims=True)
    return (x * lax.rsqrt(ms + NORM_EPS) * g) * (1.0 + scale) + shift


def _dot(a, b):
    return jnp.dot(a, b, preferred_element_type=F32)


def _dot_nt(a, b):
    return lax.dot_general(a, b, (((1,), (1,)), ((), ())), preferred_element_type=F32)


def _ada_body(c_ref, w_ref, b_ref, o_ref):
    c = c_ref[...]
    s = (c * jax.nn.sigmoid(c)).astype(BF16)
    o_ref[0] = _dot(s, w_ref[0].astype(BF16)) + b_ref[0]


def _ada_call(cond, ada_w, ada_b):
    L, D, D6 = ada_w.shape
    R = cond.shape[0]
    return pl.pallas_call(
        _ada_body,
        grid=(L, D6 // ADA_TN),
        in_specs=[
            pl.BlockSpec((R, D), lambda l, j: (0, 0)),
            pl.BlockSpec((1, D, ADA_TN), lambda l, j: (l, 0, j)),
            pl.BlockSpec((1, 1, ADA_TN), lambda l, j: (l, 0, j)),
        ],
        out_specs=pl.BlockSpec((1, R, ADA_TN), lambda l, j: (l, 0, j)),
        out_shape=jax.ShapeDtypeStruct((L, R, D6), F32),
        compiler_params=_cparams(2),
        name="ada_mod",
    )(cond, ada_w, ada_b.reshape(L, 1, D6))


def _tok_specs(B, nct, D):
    x_spec = pl.BlockSpec((1, TM, D), lambda b, t: (b, t, 0))
    mod_spec = pl.BlockSpec((1, 6, D), lambda b, t: (jnp.where(t < nct, B, b), 0, 0))
    return x_spec, mod_spec


def _da_pre_body(x_ref, mod_ref, g_ref, wqt_ref, wk_ref, wks_ref, wvt_ref, cq_ref, sq_ref, ck_ref, sk_ref,
                 qt_ref, k_ref, vt_ref):
    m = mod_ref[0]
    hb = _norm_mod(x_ref[0], g_ref[...], m[0:1], m[1:2]).astype(BF16)
    D = hb.shape[1]
    qt = _dot_nt(wqt_ref[...], hb)
    cq = cq_ref[...]
    sq = sq_ref[...]
    hd = DA_HEAD_DIM
    for h in range(D // hd):
        blk = [qt[h * hd + 16 * i:h * hd + 16 * (i + 1)] for i in range(4)]
        for i in range(4):
            r0 = 16 * i
            out = blk[i] * cq[r0:r0 + 16] + blk[i ^ 1] * sq[r0:r0 + 16]
            qt_ref[0, 0, h * hd + r0:h * hd + r0 + 16, :] = out.astype(BF16)
    k = _dot(hb, wk_ref[...])
    ks = _dot(hb, wks_ref[...])
    ck = ck_ref[...]
    sk = sk_ref[...]
    for j in range(D // 128):
        sl = slice(128 * j, 128 * (j + 1))
        k_ref[0, :, sl] = (k[:, sl] * ck + ks[:, sl] * sk).astype(BF16)
    vt_ref[0, 0] = _dot_nt(wvt_ref[...], hb).astype(BF16)


def _da_pre_call(xa, mod_i, g1, wqt, wk, wks, wvt, cq, sq, ck, sk, nct):
    B, T, D = xa.shape
    NT = T // TM
    x_spec, mod_spec = _tok_specs(B, nct, D)
    fm_spec = pl.BlockSpec((1, 1, D, TM), lambda b, t: (b, t, 0, 0))
    return pl.pallas_call(
        _da_pre_body,
        grid=(B, NT),
        in_specs=[
            x_spec, mod_spec, _const_spec((1, D)),
            _const_spec((D, D)), _const_spec((D, D)), _const_spec((D, D)), _const_spec((D, D)),
            pl.BlockSpec((64, TM), lambda b, t: (0, t)), pl.BlockSpec((64, TM), lambda b, t: (0, t)),
            pl.BlockSpec((TM, 128), lambda b, t: (t, 0)), pl.BlockSpec((TM, 128), lambda b, t: (t, 0)),
        ],
        out_specs=[fm_spec, x_spec, fm_spec],
        out_shape=[
            jax.ShapeDtypeStruct((B, NT, D, TM), BF16),
            jax.ShapeDtypeStruct((B, T, D), BF16),
            jax.ShapeDtypeStruct((B, NT, D, TM), BF16),
        ],
        compiler_params=_cparams(2),
        name="da_pre",
    )(xa, mod_i, g1, wqt, wk, wks, wvt, cq, sq, ck, sk)


def _attn_body(lam_ref, g_ref, q_ref, k_ref, v_ref, o_ref, m_ref, l_ref, acc_ref, s_ref, mx_ref, q2_ref,
               *, mode, nct, nt, lam_init, kvt, unroll):
    dk, tq = q_ref.shape[2], q_ref.shape[3]

    def prep_q(qb):
        q = q_ref[0, qb].astype(F32)
        rows = lax.broadcasted_iota(jnp.int32, (dk, tq), 0)
        if mode == "diff":
            sel_a = rows < 64
            qa = jnp.where(sel_a, q, 0.0)
            qb_ = jnp.where(sel_a, 0.0, q)
        else:
            blk = rows // 32
            in_a = (blk == 0) | (blk == 1) | (blk == 4)
            in_b = (blk == 2) | (blk == 3) | (blk == 5)
            qa = jnp.where(in_a, q, 0.0)
            qb_ = jnp.where(in_b, q, 0.0)
        q2_ref[...] = jnp.concatenate([qa, qb_], axis=1).astype(BF16)

    def init():
        m_ref[...] = jnp.full(m_ref.shape, -1e30, F32)
        l_ref[...] = jnp.zeros(l_ref.shape, F32)
        acc_ref[...] = jnp.zeros(acc_ref.shape, F32)

    def qk(c0, ntile, slot):
        tk = ntile * TM
        k0 = c0 * TM if isinstance(c0, int) else pl.multiple_of(c0 * TM, TM)
        s = _dot(k_ref[0, pl.ds(k0, tk), :], q2_ref[...])
        s_ref[slot, 0:tk, 0:2 * tq] = s
        mx_ref[slot] = jnp.max(s, axis=0, keepdims=True)

    def softmax_pv(c0, ntile, slot):
        s = s_ref[slot, 0:ntile * TM, 0:2 * tq]
        m_prev = m_ref[...]
        m_new = jnp.maximum(m_prev, mx_ref[slot])
        alpha = jnp.exp2(m_prev - m_new)
        p = jnp.exp2(s - m_new)
        l_ref[...] = alpha * l_ref[...] + jnp.sum(p, axis=0, keepdims=True)
        pb = p.astype(BF16)
        vc = jnp.concatenate([v_ref[0, c0 + j] for j in range(ntile)], axis=1)
        acc_ref[...] = alpha * acc_ref[...] + _dot(vc, pb)
        m_ref[...] = m_new

    def finish(qb):
        a = acc_ref[...] * (1.0 / l_ref[...])
        if mode == "diff":
            lv = lam_ref[...]
            lam = (jnp.exp(jnp.sum(lv[0:1] * lv[1:2], keepdims=True))
                   - jnp.exp(jnp.sum(lv[2:3] * lv[3:4], keepdims=True)) + lam_init)
            o = a[:, :tq] - lam * a[:, tq:]
            ms = jnp.mean(o * o, axis=0, keepdims=True)
            o = o * lax.rsqrt(ms + DA_SUBLN_EPS)
            ot = (o.T * g_ref[...]) * (1.0 - lam_init)
        else:
            ot = jnp.concatenate([a[:64, :tq], a[64:, tq:]], axis=0).T
        if isinstance(qb, int):
            o_ref[0, qb * TM:(qb + 1) * TM, :] = ot.astype(BF16)
        else:
            o_ref[0, pl.ds(pl.multiple_of(qb * TM, TM), TM), :] = ot.astype(BF16)

    for qb in range(nct):
        prep_q(qb)
        init()
        for c in range(nct):
            qk(c, 1, 0)
            softmax_pv(c, 1, 0)
        finish(qb)

    n = nt // kvt
    nq = nt - nct
    last = (n - 1) * kvt
    npairs = (n - 1) // 2
    if npairs % unroll:
        unroll = 1

    def tile_core(slot0):
        init()

        def pairs(j, carry):
            for u in range(unroll):
                c = 2 * (j * unroll + u) * kvt
                qk(c + kvt, kvt, 1 - slot0)
                softmax_pv(c, kvt, slot0)
                qk(c + 2 * kvt, kvt, slot0)
                softmax_pv(c + kvt, kvt, 1 - slot0)
            return carry

        lax.fori_loop(0, npairs // unroll + jnp.minimum(pl.program_id(0), 0), pairs, 0)

    def tile_end(qb, slot0, qb_next):
        prep_q(qb_next)
        qk(0, kvt, 1 - slot0)
        softmax_pv(last, kvt, slot0)
        finish(qb)

    prep_q(nct)
    qk(0, kvt, 0)

    def two_tiles(i, carry):
        qb = nct + 2 * i
        tile_core(0)
        tile_end(qb, 0, qb + 1)
        tile_core(1)
        tile_end(qb + 1, 1, jnp.minimum(qb + 2, nt - 1))
        return carry

    lax.fori_loop(0, nq // 2, two_tiles, 0)
    if nq % 2:
        tile_core(0)
        softmax_pv(last, kvt, 0)
        finish(nt - 1)


def _attn_call(lam, g, qt, k, vt, *, mode, nct, lam_init, kvt=KV_TILES, unroll=PAIR_UNROLL):
    B, NT, GD, _ = qt.shape
    T = NT * TM
    G = vt.shape[2] // 128
    dk = GD // G
    body = functools.partial(_attn_body, mode=mode, nct=nct, nt=NT, lam_init=lam_init, kvt=kvt, unroll=unroll)
    return pl.pallas_call(
        body,
        grid=(B, G),
        in_specs=[
            _const_spec(lam.shape), _const_spec(g.shape),
            pl.BlockSpec((1, NT, dk, TM), lambda b, h: (b, 0, h, 0)),
            pl.BlockSpec((1, T, dk), lambda b, h: (b, 0, h)),
            pl.BlockSpec((1, NT, 128, TM), lambda b, h: (b, 0, h, 0)),
        ],
        out_specs=pl.BlockSpec((1, T, 128), lambda b, h: (b, 0, h)),
        out_shape=jax.ShapeDtypeStruct((B, T, G * 128), BF16),
        scratch_shapes=[
            pltpu.VMEM((1, 2 * TM), F32), pltpu.VMEM((1, 2 * TM), F32), pltpu.VMEM((128, 2 * TM), F32),
            pltpu.VMEM((2, kvt * TM, 2 * TM), F32), pltpu.VMEM((2, 1, 2 * TM), F32),
            pltpu.VMEM((dk, 2 * TM), BF16),
        ],
        compiler_params=_cparams(2),
        name="attn_" + mode,
    )(lam, g, qt, k, vt)


def _post_body(*refs, has_gate):
    if has_gate:
        x_ref, o_ref, gt_ref, mod_ref, wo_ref, n2_ref, w1_ref, w3_ref, w2_ref, out_ref = refs
        o = (o_ref[0] * gt_ref[0]).astype(BF16)
    else:
        x_ref, o_ref, mod_ref, wo_ref, n2_ref, w1_ref, w3_ref, w2_ref, out_ref = refs
        o = o_ref[0]
    m = mod_ref[0]
    x1 = x_ref[0] + m[2:3] * _dot(o, wo_ref[...])
    h = _norm_mod(x1, n2_ref[...], m[3:4], m[4:5]).astype(BF16)
    a = _dot(h, w1_ref[...])
    b = _dot(h, w3_ref[...])
    u = (a * jax.nn.sigmoid(a) * b).astype(BF16)
    out_ref[0] = x1 + m[5:6] * _dot(u, w2_ref[...])


def _post_call(xa, o, gate, mod_i, wo, n2, w1, w3, w2, nct):
    B, T, D = xa.shape
    F = w1.shape[1]
    x_spec, mod_spec = _tok_specs(B, nct, D)
    has_gate = gate is not None
    ins = [xa, o] + ([gate] if has_gate else []) + [mod_i, wo, n2, w1, w3, w2]
    one = pl.Buffered(1)
    specs = [x_spec, x_spec] + ([x_spec] if has_gate else []) + [
        mod_spec,
        pl.BlockSpec((D, D), lambda b, t: (0, 0), pipeline_mode=one),
        _const_spec((1, D)),
        pl.BlockSpec((D, F), lambda b, t: (0, 0), pipeline_mode=one),
        pl.BlockSpec((D, F), lambda b, t: (0, 0), pipeline_mode=one),
        pl.BlockSpec((F, D), lambda b, t: (0, 0), pipeline_mode=one),
    ]
    return pl.pallas_call(
        functools.partial(_post_body, has_gate=has_gate),
        grid=(B, T // TM),
        in_specs=specs,
        out_specs=x_spec,
        out_shape=jax.ShapeDtypeStruct((B, T, D), F32),
        input_output_aliases={0: 0},
        compiler_params=_cparams(2),
        name="post_ffn",
    )(*ins)


def _mla_pre_body(x_ref, mod_ref, g_ref, wext_ref, gq_ref, gkv_ref, wuqt_ref, wkn_ref, wvt_ref,
                  cq_ref, sq_ref, ck_ref, sk_ref, qt_ref, k_ref, vt_ref, *, scale):
    m = mod_ref[0]
    hb = _norm_mod(x_ref[0], g_ref[...], m[0:1], m[1:2]).astype(BF16)
    down = _dot(hb, wext_ref[...])
    ql = down[:, :MLA_Q_LORA]
    cq_lat = (ql * lax.rsqrt(jnp.mean(ql * ql, axis=-1, keepdims=True) + NORM_EPS) * gq_ref[...]).astype(BF16)
    kl = down[:, MLA_Q_LORA:MLA_Q_LORA + MLA_KV_LORA]
    ckv = (kl * lax.rsqrt(jnp.mean(kl * kl, axis=-1, keepdims=True) + NORM_EPS) * gkv_ref[...]).astype(BF16)
    kr = (down[:, 384:512] * ck_ref[...] + down[:, 512:640] * sk_ref[...]).astype(BF16)
    kn = _dot(ckv, wkn_ref[...])
    npair = kn.shape[1] // 128
    for p in range(npair):
        k_ref[0, :, 256 * p:256 * p + 128] = kn[:, 128 * p:128 * (p + 1)].astype(BF16)
        k_ref[0, :, 256 * p + 128:256 * (p + 1)] = kr
    vt_ref[0, 0] = _dot_nt(wvt_ref[...], ckv).astype(BF16)
    qt = _dot_nt(wuqt_ref[...], cq_lat)
    cq = cq_ref[...]
    sq = sq_ref[...]
    for p in range(npair):
        base = 256 * p
        qt_ref[0, 0, base:base + 128, :] = (qt[base:base + 128] * scale).astype(BF16)
        blk = [qt[base + 128 + 8 * i:base + 136 + 8 * i] for i in range(8)]
        for i in range(8):
            out = blk[i] * cq[8 * i:8 * i + 8] + blk[i ^ 1] * sq[8 * i:8 * i + 8]
            qt_ref[0, 0, base + 128 + 8 * i:base + 136 + 8 * i, :] = out.astype(BF16)
        qt_ref[0, 0, base + 192:base + 256, :] = jnp.zeros((64, qt.shape[1]), BF16)


def _mla_pre_call(xa, mod_i, g1, wext, gq, gkv, wuqt, wkn, wvt, cq, sq, ck, sk, nct, scale):
    B, T, D = xa.shape
    NT = T // TM
    x_spec, mod_spec = _tok_specs(B, nct, D)
    GD = wuqt.shape[0]
    return pl.pallas_call(
        functools.partial(_mla_pre_body, scale=scale),
        grid=(B, NT),
        in_specs=[
            x_spec, mod_spec, _const_spec((1, D)),
            _const_spec(wext.shape), _const_spec(gq.shape), _const_spec(gkv.shape),
            _const_spec(wuqt.shape), _const_spec(wkn.shape), _const_spec(wvt.shape),
            pl.BlockSpec((64, TM), lambda b, t: (0, t)), pl.BlockSpec((64, TM), lambda b, t: (0, t)),
            pl.BlockSpec((TM, 128), lambda b, t: (t, 0)), pl.BlockSpec((TM, 128), lambda b, t: (t, 0)),
        ],
        out_specs=[
            pl.BlockSpec((1, 1, GD, TM), lambda b, t: (b, t, 0, 0)),
            pl.BlockSpec((1, TM, GD), lambda b, t: (b, t, 0)),
            pl.BlockSpec((1, 1, D, TM), lambda b, t: (b, t, 0, 0)),
        ],
        out_shape=[
            jax.ShapeDtypeStruct((B, NT, GD, TM), BF16),
            jax.ShapeDtypeStruct((B, T, GD), BF16),
            jax.ShapeDtypeStruct((B, NT, D, TM), BF16),
        ],
        compiler_params=_cparams(2),
        name="mla_pre",
    )(xa, mod_i, g1, wext, gq, gkv, wuqt, wkn, wvt, cq, sq, ck, sk)


def _rw_pre_body(x_ref, xp_ref, xn_ref, mod_ref, g_ref, mu_ref, wr_ref, wk_ref, wv_ref, w1_ref, w2_ref, w0_ref,
                 a1_ref, a2_ref, a0_ref, g1_ref, g2_ref, r_ref, k_ref, v_ref, w_ref, a_ref, gt_ref, *, nct, nt):
    t = pl.program_id(1)
    m = mod_ref[0]
    g = g_ref[...]
    h = _norm_mod(x_ref[0], g, m[0:1], m[1:2])
    hp = _norm_mod(xp_ref[0], g, m[0:1], m[1:2])[7:8]
    hn = _norm_mod(xn_ref[0], g, m[0:1], m[1:2])[0:1]
    has_prev = jnp.logical_and(t != 0, t != nct)
    has_next = jnp.logical_and(t != nct - 1, t != nt - 1)
    hp = jnp.where(has_prev, hp, 0.0)
    hn = jnp.where(has_next, hn, 0.0)
    tm = h.shape[0]
    rid = lax.broadcasted_iota(jnp.int32, h.shape, 0)
    h_m1 = jnp.where(rid == 0, hp, pltpu.roll(h, 1, axis=0))
    h_p1 = jnp.where(rid == tm - 1, hn, pltpu.roll(h, tm - 1, axis=0))
    cs = 0.5 * (h_m1 + h_p1) - h
    mu = mu_ref[...]

    def mix(s):
        return (h + cs * mu[s:s + 1]).astype(BF16)

    r_ref[0] = _dot(mix(0), wr_ref[...])
    k_ref[0] = _dot(mix(1), wk_ref[...])
    v_ref[0] = _dot(mix(2), wv_ref[...])
    lw = jnp.tanh(_dot(mix(3), w1_ref[...])).astype(BF16)
    la = _dot(mix(4), a1_ref[...]).astype(BF16)
    for z in range(2):
        u = -(w0_ref[z:z + 1] + _dot(lw, w2_ref[z]))
        sp = jnp.maximum(u, 0.0) + jnp.log(1.0 + jnp.exp(-jnp.abs(u)))
        w_ref[z, 0] = jnp.exp(-jnp.exp(-sp - 0.5))
        a_ref[z, 0] = jax.nn.sigmoid(a0_ref[z:z + 1] + _dot(la, a2_ref[z]))
    gl = jax.nn.sigmoid(_dot(mix(5), g1_ref[...])).astype(BF16)
    gt_ref[0] = _dot(gl, g2_ref[...])


def _rw_pre_call(xa, mod_i, g1n, mu, wr, wk, wv, w1, w2, w0, a1, a2, a0, g1, g2, nct):
    B, T, D = xa.shape
    NT = T // TM
    x_spec, mod_spec = _tok_specs(B, nct, D)
    r8 = TM // 8
    prev_spec = pl.BlockSpec((1, 8, D), lambda b, t: (b, jnp.maximum(t * r8 - 1, 0), 0))
    next_spec = pl.BlockSpec((1, 8, D), lambda b, t: (b, jnp.minimum((t + 1) * r8, T // 8 - 1), 0))
    dir_spec = pl.BlockSpec((2, 1, TM, D), lambda b, t: (0, b, t, 0))
    consts = [g1n, mu, wr, wk, wv, w1, w2, w0, a1, a2, a0, g1, g2]
    tok = jax.ShapeDtypeStruct((B, T, D), F32)
    two = jax.ShapeDtypeStruct((2, B, T, D), F32)
    return pl.pallas_call(
        functools.partial(_rw_pre_body, nct=nct, nt=NT),
        grid=(B, NT),
        in_specs=[x_spec, prev_spec, next_spec, mod_spec] + [_const_spec(c.shape) for c in consts],
        out_specs=[x_spec, x_spec, x_spec, dir_spec, dir_spec, x_spec],
        out_shape=[tok, tok, tok, two, two, tok],
        compiler_params=_cparams(2),
        name="rw_pre",
    )(xa, xa, xa, mod_i, *consts)


def _rw_scan_body(r_ref, k_ref, v_ref, w_ref, a_ref, kk_ref, ka_ref, y_ref, s_ref, *, tc):
    d = pl.program_id(0)

    @pl.when(pl.program_id(1) == 0)
    def _():
        s_ref[...] = jnp.zeros(s_ref.shape, F32)

    nv, nk, bh = s_ref.shape
    rowid = lax.broadcasted_iota(jnp.int32, (8, bh), 0)
    low4, low2, low1 = (rowid & 4) == 0, (rowid & 2) == 0, (rowid & 1) == 0

    def fold(x, y, keep, dist):
        u = jnp.where(keep, x, y)
        w = jnp.where(keep, y, x)
        if dist == 4:
            return u + pltpu.roll(w, 4, axis=0)
        return u + jnp.where(keep, pltpu.roll(w, 8 - dist, axis=0), pltpu.roll(w, dist, axis=0))

    def row_sums(parts):
        z = [fold(parts[i], parts[i + 4], low4, 4) for i in range(4)]
        y = [fold(z[i], z[i + 2], low2, 2) for i in range(2)]
        return fold(y[0], y[1], low1, 1)

    def step(i, p):
        te = jnp.where(d == 0, i, tc - 1 - i)
        r = r_ref[te]
        k = k_ref[te]
        a = a_ref[0, te]
        kk = k * kk_ref[...]
        nrm = jnp.sqrt(jnp.sum(kk * kk, axis=0, keepdims=True))
        kk = kk / jnp.maximum(nrm, 1e-12)
        pn = p * w_ref[0, te]
        ip = 1.0 / pn
        at = -kk * p
        bt = kk * a * ip
        kt = k * (1.0 + (a - 1.0) * ka_ref[...]) * ip
        rt = r * pn
        for g in range(nv // 8):
            parts = []
            for j in range(8):
                vi = 8 * g + j
                s = s_ref[vi]
                sa = jnp.sum(s * at, axis=0, keepdims=True)
                vv = v_ref[te, pl.ds(vi, 1), :]
                sn = s + sa * bt + vv * kt
                s_ref[vi] = sn
                parts.append(jnp.sum((sn * rt).reshape(nk // 8, 8, bh), axis=0))
            y_ref[0, te, 8 * g:8 * g + 8, :] = row_sums(parts)
        return pn

    p_end = lax.fori_loop(0, tc, step, jnp.ones((nk, bh), F32))
    for vi in range(nv):
        s_ref[vi] = s_ref[vi] * p_end


def _rw_scan_call(r, k, v, w, a, kk_tab, ka_tab, ncc):
    T, K, BH = r.shape
    tc = SCAN_TC
    NC = T // tc

    def cidx(d, c):
        bwd = jnp.where(c < ncc, ncc - 1 - c, NC - 1 - (c - ncc))
        return jnp.where(d == 0, c, bwd)

    sh_spec = pl.BlockSpec((tc, K, BH), lambda d, c: (cidx(d, c), 0, 0))
    dir_spec = pl.BlockSpec((1, tc, K, BH), lambda d, c: (d, cidx(d, c), 0, 0))
    return pl.pallas_call(
        functools.partial(_rw_scan_body, tc=tc),
        grid=(2, NC),
        in_specs=[sh_spec, sh_spec, sh_spec, dir_spec, dir_spec, _const_spec((K, BH)), _const_spec((K, BH))],
        out_specs=dir_spec,
        out_shape=jax.ShapeDtypeStruct((2, T, K, BH), F32),
        scratch_shapes=[pltpu.VMEM((K, K, BH), F32)],
        compiler_params=_cparams(2),
        name="rw_scan",
    )(r, k, v, w, a, kk_tab, ka_tab)


def _rw_fin_body(y_ref, r_ref, k_ref, v_ref, a_ref, ka_ref, rk_ref, lg_ref, lb_ref, z_ref):
    y = y_ref[0] + y_ref[1]
    mean = jnp.mean(y, axis=1, keepdims=True)
    yc = y - mean
    var = jnp.mean(yc * yc, axis=1, keepdims=True)
    yn = yc * lax.rsqrt(var + RW_GN_EPS) * lg_ref[...] + lb_ref[...]
    k = k_ref[...]
    ka = ka_ref[...]
    ksum = k * (1.0 + (a_ref[0] - 1.0) * ka) + k * (1.0 + (a_ref[1] - 1.0) * ka)
    bonus = jnp.sum(r_ref[...] * ksum * rk_ref[...], axis=1, keepdims=True) * v_ref[...]
    z_ref[...] = yn + bonus


def _rw_fin_call(y, r, k, v, a, ka_tab, rk_tab, lg_tab, lb_tab):
    T, K, BH = r.shape
    tc = SCAN_TC
    sh_spec = pl.BlockSpec((tc, K, BH), lambda c: (c, 0, 0))
    dir_spec = pl.BlockSpec((2, tc, K, BH), lambda c: (0, c, 0, 0))
    tab = _const_spec((K, BH))
    return pl.pallas_call(
        _rw_fin_body,
        grid=(T // tc,),
        in_specs=[dir_spec, sh_spec, sh_spec, sh_spec, dir_spec, tab, tab, tab, tab],
        out_specs=sh_spec,
        out_shape=jax.ShapeDtypeStruct((T, K, BH), F32),
        compiler_params=_cparams(1),
        name="rw_fin",
    )(y, r, k, v, a, ka_tab, rk_tab, lg_tab, lb_tab)


def _final_body(x_ref, g_ref, o_ref):
    x = x_ref[0]
    ms = jnp.mean(x * x, axis=-1, keepdims=True)
    o_ref[0] = x * lax.rsqrt(ms + NORM_EPS) * g_ref[...]


def _final_call(xa, g, nct):
    B, T, D = xa.shape
    N = T - nct * TM
    return pl.pallas_call(
        _final_body,
        grid=(B, N // TM),
        in_specs=[pl.BlockSpec((1, TM, D), lambda b, t: (b, t + nct, 0)), _const_spec((1, D))],
        out_specs=pl.BlockSpec((1, TM, D), lambda b, t: (b, t, 0)),
        out_shape=jax.ShapeDtypeStruct((B, N, D), F32),
        compiler_params=_cparams(2),
        name="final_norm",
    )(xa, g)


def _rope_tables(n_ctx, n_lat, rot_dim):
    mfreq = rot_dim // 4
    t = jnp.arange(n_lat)
    row = (t // GRID_W).astype(F32)
    col = (t % GRID_W).astype(F32)
    inv = ROPE_BASE ** (-jnp.arange(mfreq, dtype=F32) / mfreq)
    ar = row[:, None] * inv
    ac = col[:, None] * inv
    cos = jnp.concatenate([jnp.cos(ar), jnp.cos(ar), jnp.cos(ac), jnp.cos(ac)], axis=1)
    sin = jnp.concatenate([-jnp.sin(ar), jnp.sin(ar), -jnp.sin(ac), jnp.sin(ac)], axis=1)
    cos = jnp.concatenate([jnp.ones((n_ctx, rot_dim), F32), cos], axis=0)
    sin = jnp.concatenate([jnp.zeros((n_ctx, rot_dim), F32), sin], axis=0)
    return cos, sin


def _swap_perm(n, half):
    idx = jnp.arange(n)
    return idx ^ half


def _to_scan(x, B):
    lead = x.shape[:-3]
    T, D = x.shape[-2:]
    H = D // RW_HEAD
    x = x.reshape(lead + (B, T, H, RW_HEAD))
    nl = len(lead)
    perm = tuple(range(nl)) + (nl + 1, nl + 3, nl, nl + 2)
    return jnp.transpose(x, perm).reshape(lead + (T, RW_HEAD, B * H))


def _head_tab(p, B):
    H = p.size // RW_HEAD
    t = p.reshape(H, RW_HEAD).T
    return jnp.tile(t[:, None, :], (1, B, 1)).reshape(RW_HEAD, B * H).astype(F32)


def kernel(x, c, ctx, c_ctx, ada_w, ada_b, norm1_g, norm2_g, ffn_w1, ffn_w3, ffn_w2, final_norm_g, da_wqkv, da_lambda, da_subln_g, da_wo, rw_mu, rw_wrkv, rw_w0, rw_w1, rw_w2, rw_a0, rw_a1, rw_a2, rw_g1, rw_g2, rw_k_k, rw_k_a, rw_r_k, rw_lnx_g, rw_lnx_b, rw_wo, mla_wdown, mla_q_norm_g, mla_wuq, mla_kv_norm_g, mla_wukv, mla_wo):
    B, N, D = x.shape
    n_ctx = ctx.shape[1]
    depth = ada_w.shape[0]
    assert n_ctx % TM == 0 and N % TM == 0 and n_ctx % SCAN_TC == 0
    nct = n_ctx // TM
    T = n_ctx + N
    assert (T // TM) % KV_TILES == 0 and (T // TM // KV_TILES) % 2 == 1

    xa = jnp.concatenate([ctx, x], axis=1)
    R = -(-(B + 1) // 8) * 8
    cond = jnp.zeros((R, D), F32).at[:B].set(c).at[B].set(c_ctx)
    mod = _ada_call(cond, ada_w, ada_b).reshape(depth, R, 6, D)

    da_cos, da_sin = _rope_tables(n_ctx, N, DA_HEAD_DIM)
    da_scale = (DA_HEAD_DIM ** -0.5) * LOG2E
    da_cq, da_sq = (da_cos * da_scale).T, (da_sin * da_scale).T
    da_ck, da_sk = jnp.tile(da_cos, (1, 2)), jnp.tile(da_sin, (1, 2))
    ml_cos, ml_sin = _rope_tables(n_ctx, N, MLA_ROPE)
    ml_scale = ((MLA_NOPE + MLA_ROPE) ** -0.5) * LOG2E
    ml_cq, ml_sq = jnp.tile((ml_cos * ml_scale).T, (2, 1)), jnp.tile((ml_sin * ml_scale).T, (2, 1))
    zpad = jnp.zeros((T, 64), F32)
    ml_ck = jnp.concatenate([ml_cos, ml_cos, zpad], axis=1)
    ml_sk = jnp.concatenate([ml_sin, ml_sin, zpad], axis=1)

    for i in range(depth):
        kind, j = i % N_MIXERS, i // N_MIXERS
        mod_i = mod[i]
        g1 = norm1_g[i].reshape(1, D)
        gate = None
        if kind == 0:
            wq, wk, wv = jnp.split(da_wqkv[j], 3, axis=1)
            perm = _swap_perm(D, 16)
            qt, kk, vt = _da_pre_call(xa, mod_i, g1, wq.T.astype(BF16), wk.astype(BF16), wk[:, perm].astype(BF16),
                                      wv.T.astype(BF16), da_cq, da_sq, da_ck, da_sk, nct)
            lam_init = 0.8 - 0.6 * math.exp(-0.3 * i)
            o = _attn_call(da_lambda[j], da_subln_g[j].reshape(1, -1), qt, kk, vt,
                           mode="diff", nct=nct, lam_init=lam_init)
            wo = da_wo[j]
        elif kind == 1:
            H = D // RW_HEAD
            w1c = jnp.concatenate([rw_w1[j, 0], rw_w1[j, 1]], axis=1).astype(BF16)
            a1c = jnp.concatenate([rw_a1[j, 0], rw_a1[j, 1]], axis=1).astype(BF16)
            zl = jnp.zeros_like(rw_w2[j, 0])
            w2p = jnp.stack([jnp.concatenate([rw_w2[j, 0], zl], axis=0),
                             jnp.concatenate([zl, rw_w2[j, 1]], axis=0)]).astype(BF16)
            a2p = jnp.stack([jnp.concatenate([rw_a2[j, 0], zl], axis=0),
                             jnp.concatenate([zl, rw_a2[j, 1]], axis=0)]).astype(BF16)
            gl = rw_g1.shape[2]
            glp = -(-gl // 128) * 128
            g1p = jnp.pad(rw_g1[j], ((0, 0), (0, glp - gl))).astype(BF16)
            g2p = jnp.pad(rw_g2[j], ((0, glp - gl), (0, 0))).astype(BF16)
            r, k, v, w, a, gate = _rw_pre_call(
                xa, mod_i, g1, rw_mu[j], rw_wrkv[j, 0].astype(BF16), rw_wrkv[j, 1].astype(BF16),
                rw_wrkv[j, 2].astype(BF16), w1c, w2p, rw_w0[j], a1c, a2p, rw_a0[j], g1p, g2p, nct)
            rs, ks, vs, ws, as_ = (_to_scan(u, B) for u in (r, k, v, w, a))
            ka_tab = _head_tab(rw_k_a[j], B)
            y = _rw_scan_call(rs, ks, vs, ws, as_, _head_tab(rw_k_k[j], B), ka_tab, n_ctx // SCAN_TC)
            z = _rw_fin_call(y, rs, ks, vs, as_, ka_tab, _head_tab(rw_r_k[j], B),
                             _head_tab(rw_lnx_g[j], B), _head_tab(rw_lnx_b[j], B))
            o = jnp.transpose(z.reshape(T, RW_HEAD, B, H), (2, 0, 3, 1)).reshape(B, T, D)
            wo = rw_wo[j]
        else:
            wd = mla_wdown[j]
            nlat = MLA_Q_LORA + MLA_KV_LORA
            kr = wd[:, nlat:]
            krs = kr[:, _swap_perm(MLA_ROPE, 8)]
            zc = jnp.zeros((D, 64), F32)
            wext = jnp.concatenate([wd[:, :nlat], kr, kr, zc, krs, krs, zc], axis=1).astype(BF16)
            wuq = mla_wuq[j].reshape(MLA_Q_LORA, MLA_HEADS // 2, 2, MLA_NOPE + MLA_ROPE)
            qn = wuq[..., :MLA_NOPE].reshape(MLA_Q_LORA, MLA_HEADS // 2, 2 * MLA_NOPE)
            qr = wuq[..., MLA_NOPE:].reshape(MLA_Q_LORA, MLA_HEADS // 2, 2 * MLA_ROPE)
            qz = jnp.zeros((MLA_Q_LORA, MLA_HEADS // 2, 64), F32)
            wuqt = jnp.concatenate([qn, qr, qz], axis=2).reshape(MLA_Q_LORA, -1).T.astype(BF16)
            wukv = mla_wukv[j].reshape(MLA_KV_LORA, MLA_HEADS, MLA_NOPE + MLA_V)
            wkn = wukv[..., :MLA_NOPE].reshape(MLA_KV_LORA, -1).astype(BF16)
            wvt = wukv[..., MLA_NOPE:].reshape(MLA_KV_LORA, -1).T.astype(BF16)
            qt, kk, vt = _mla_pre_call(xa, mod_i, g1, wext, mla_q_norm_g[j].reshape(1, -1),
                                       mla_kv_norm_g[j].reshape(1, -1), wuqt, wkn, wvt,
                                       ml_cq, ml_sq, ml_ck, ml_sk, nct, ml_scale)
            o = _attn_call(jnp.zeros((4, 64), F32), jnp.ones((1, 128), F32), qt, kk, vt,
                           mode="mla", nct=nct, lam_init=0.0)
            wo = mla_wo[j]
        xa = _post_call(xa, o, gate, mod_i, wo.astype(BF16), norm2_g[i].reshape(1, D),
                        ffn_w1[i].astype(BF16), ffn_w3[i].astype(BF16), ffn_w2[i].astype(BF16), nct)
    return _final_call(xa, final_norm_g.reshape(1, D), nct)
```

```python
import functools
import math

import jax
import jax.numpy as jnp
from jax import lax
from jax.experimental import pallas as pl
from jax.experimental.pallas import tpu as pltpu

F32 = jnp.float32
BF16 = jnp.bfloat16

GRID_W = 64
ROPE_BASE = 10000.0
NORM_EPS = 1e-6
N_MIXERS = 3
DA_HEAD_DIM = 64
DA_SUBLN_EPS = 1e-5
RW_HEAD = 64
RW_GN_EPS = 64e-5
MLA_HEADS = 16
MLA_Q_LORA = 256
MLA_KV_LORA = 128
MLA_NOPE = 64
MLA_ROPE = 32
MLA_V = 64
LOG2E = math.log2(math.e)

TM = 256
KV_TILES = 3
PAIR_UNROLL = 5
SCAN_TC = 32
ADA_TN = 1536
VMEM_LIMIT = 56 * 1024 * 1024


def _cparams(n_grid):
    return pltpu.CompilerParams(dimension_semantics=("arbitrary",) * n_grid, vmem_limit_bytes=VMEM_LIMIT)


def _const_spec(shape):
    nd = len(shape)
    return pl.BlockSpec(shape, lambda *_: (0,) * nd)


def _norm_mod(x, g, shift, scale):
    ms = jnp.mean(x * x, axis=-1, keepdims=True)
    return (x * lax.rsqrt(ms + NORM_EPS) * g) * (1.0 + scale) + shift


def _dot(a, b):
    return jnp.dot(a, b, preferred_element_type=F32)


def _dot_nt(a, b):
    return lax.dot_general(a, b, (((1,), (1,)), ((), ())), preferred_element_type=F32)


def _ada_body(c_ref, w_ref, b_ref, o_ref):
    c = c_ref[...]
    s = (c * jax.nn.sigmoid(c)).astype(BF16)
    o_ref[0] = _dot(s, w_ref[0].astype(BF16)) + b_ref[0]


def _ada_call(cond, ada_w, ada_b):
    L, D, D6 = ada_w.shape
    R = cond.shape[0]
    return pl.pallas_call(
        _ada_body,
        grid=(L, D6 // ADA_TN),
        in_specs=[
            pl.BlockSpec((R, D), lambda l, j: (0, 0)),
            pl.BlockSpec((1, D, ADA_TN), lambda l, j: (l, 0, j)),
            pl.BlockSpec((1, 1, ADA_TN), lambda l, j: (l, 0, j)),
        ],
        out_specs=pl.BlockSpec((1, R, ADA_TN), lambda l, j: (l, 0, j)),
        out_shape=jax.ShapeDtypeStruct((L, R, D6), F32),
        compiler_params=_cparams(2),
        name="ada_mod",
    )(cond, ada_w, ada_b.reshape(L, 1, D6))


def _tok_specs(B, nct, D):
    x_spec = pl.BlockSpec((1, TM, D), lambda b, t: (b, t, 0))
    mod_spec = pl.BlockSpec((1, 6, D), lambda b, t: (jnp.where(t < nct, B, b), 0, 0))
    return x_spec, mod_spec


def _da_pre_body(x_ref, mod_ref, g_ref, wqt_ref, wk_ref, wvt_ref, cq_ref, sq_ref, ck_ref, sk_ref,
                 qt_ref, k_ref, vt_ref):
    m = mod_ref[0]
    hb = _norm_mod(x_ref[0], g_ref[...], m[0:1], m[1:2]).astype(BF16)
    D = hb.shape[1]
    qt = _dot_nt(wqt_ref[...], hb)
    cq = cq_ref[...]
    sq = sq_ref[...]
    hd = DA_HEAD_DIM
    for h in range(D // hd):
        blk = [qt[h * hd + 16 * i:h * hd + 16 * (i + 1)] for i in range(4)]
        for i in range(4):
            r0 = 16 * i
            out = blk[i] * cq[r0:r0 + 16] + blk[i ^ 1] * sq[r0:r0 + 16]
            qt_ref[0, 0, h * hd + r0:h * hd + r0 + 16, :] = out.astype(BF16)
    k = _dot(hb, wk_ref[...])
    ck = ck_ref[...]
    sk = sk_ref[...]
    lane = lax.broadcasted_iota(jnp.int32, ck.shape, 1)
    first_half = (lane & 16) == 0
    for j in range(D // 128):
        kj = k[:, 128 * j:128 * (j + 1)]
        ks = jnp.where(first_half, pltpu.roll(kj, 112, axis=1), pltpu.roll(kj, 16, axis=1))
        k_ref[0, :, 128 * j:128 * (j + 1)] = (kj * ck + ks * sk).astype(BF16)
    vt_ref[0, 0] = _dot_nt(wvt_ref[...], hb).astype(BF16)


def _da_pre_call(xa, mod_i, g1, wqt, wk, wvt, cq, sq, ck, sk, nct):
    B, T, D = xa.shape
    NT = T // TM
    x_spec, mod_spec = _tok_specs(B, nct, D)
    fm_spec = pl.BlockSpec((1, 1, D, TM), lambda b, t: (b, t, 0, 0))
    return pl.pallas_call(
        _da_pre_body,
        grid=(B, NT),
        in_specs=[
            x_spec, mod_spec, _const_spec((1, D)),
            _const_spec((D, D)), _const_spec((D, D)), _const_spec((D, D)),
            pl.BlockSpec((64, TM), lambda b, t: (0, t)), pl.BlockSpec((64, TM), lambda b, t: (0, t)),
            pl.BlockSpec((TM, 128), lambda b, t: (t, 0)), pl.BlockSpec((TM, 128), lambda b, t: (t, 0)),
        ],
        out_specs=[fm_spec, x_spec, fm_spec],
        out_shape=[
            jax.ShapeDtypeStruct((B, NT, D, TM), BF16),
            jax.ShapeDtypeStruct((B, T, D), BF16),
            jax.ShapeDtypeStruct((B, NT, D, TM), BF16),
        ],
        compiler_params=_cparams(2),
        name="da_pre",
    )(xa, mod_i, g1, wqt, wk, wvt, cq, sq, ck, sk)


def _attn_body(lam_ref, g_ref, q_ref, k_ref, v_ref, o_ref, m_ref, l_ref, acc_ref, s_ref, mx_ref, q2_ref,
               *, mode, nct, nt, lam_init, kvt, unroll):
    dk, tq = q_ref.shape[2], q_ref.shape[3]

    def prep_q(qb):
        q = q_ref[0, qb].astype(F32)
        rows = lax.broadcasted_iota(jnp.int32, (dk, tq), 0)
        if mode == "diff":
            sel_a = rows < 64
            qa = jnp.where(sel_a, q, 0.0)
            qb_ = jnp.where(sel_a, 0.0, q)
        else:
            blk = rows // 32
            in_a = (blk == 0) | (blk == 1) | (blk == 4)
            in_b = (blk == 2) | (blk == 3) | (blk == 5)
            qa = jnp.where(in_a, q, 0.0)
            qb_ = jnp.where(in_b, q, 0.0)
        q2_ref[...] = jnp.concatenate([qa, qb_], axis=1).astype(BF16)

    def init():
        m_ref[...] = jnp.full(m_ref.shape, -1e30, F32)
        l_ref[...] = jnp.zeros(l_ref.shape, F32)
        acc_ref[...] = jnp.zeros(acc_ref.shape, F32)

    def qk(c0, ntile, slot):
        tk = ntile * TM
        k0 = c0 * TM if isinstance(c0, int) else pl.multiple_of(c0 * TM, TM)
        s = _dot(k_ref[0, pl.ds(k0, tk), :], q2_ref[...])
        s_ref[slot, 0:tk, 0:2 * tq] = s
        mx_ref[slot] = jnp.max(s, axis=0, keepdims=True)

    def softmax_pv(c0, ntile, slot):
        s = s_ref[slot, 0:ntile * TM, 0:2 * tq]
        m_prev = m_ref[...]
        m_new = jnp.maximum(m_prev, mx_ref[slot])
        alpha = jnp.exp2(m_prev - m_new)
        p = jnp.exp2(s - m_new)
        l_ref[...] = alpha * l_ref[...] + jnp.sum(p, axis=0, keepdims=True)
        pb = p.astype(BF16)
        vc = jnp.concatenate([v_ref[0, c0 + j] for j in range(ntile)], axis=1)
        acc_ref[...] = alpha * acc_ref[...] + _dot(vc, pb)
        m_ref[...] = m_new

    def finish(qb):
        a = acc_ref[...] * (1.0 / l_ref[...])
        if mode == "diff":
            lv = lam_ref[...]
            lam = (jnp.exp(jnp.sum(lv[0:1] * lv[1:2], keepdims=True))
                   - jnp.exp(jnp.sum(lv[2:3] * lv[3:4], keepdims=True)) + lam_init)
            o = a[:, :tq] - lam * a[:, tq:]
            ms = jnp.mean(o * o, axis=0, keepdims=True)
            o = o * lax.rsqrt(ms + DA_SUBLN_EPS)
            ot = (o.T * g_ref[...]) * (1.0 - lam_init)
        else:
            ot = jnp.concatenate([a[:64, :tq], a[64:, tq:]], axis=0).T
        if isinstance(qb, int):
            o_ref[0, qb * TM:(qb + 1) * TM, :] = ot.astype(BF16)
        else:
            o_ref[0, pl.ds(pl.multiple_of(qb * TM, TM), TM), :] = ot.astype(BF16)

    for qb in range(nct):
        prep_q(qb)
        init()
        for c in range(nct):
            qk(c, 1, 0)
            softmax_pv(c, 1, 0)
        finish(qb)

    n = nt // kvt
    nq = nt - nct
    last = (n - 1) * kvt
    npairs = (n - 1) // 2
    if npairs % unroll:
        unroll = 1

    def tile_core(slot0):
        init()

        def pairs(j, carry):
            for u in range(unroll):
                c = 2 * (j * unroll + u) * kvt
                qk(c + kvt, kvt, 1 - slot0)
                softmax_pv(c, kvt, slot0)
                qk(c + 2 * kvt, kvt, slot0)
                softmax_pv(c + kvt, kvt, 1 - slot0)
            return carry

        lax.fori_loop(0, npairs // unroll + jnp.minimum(pl.program_id(0), 0), pairs, 0)

    def tile_end(qb, slot0, qb_next):
        prep_q(qb_next)
        qk(0, kvt, 1 - slot0)
        softmax_pv(last, kvt, slot0)
        finish(qb)

    prep_q(nct)
    qk(0, kvt, 0)

    def two_tiles(i, carry):
        qb = nct + 2 * i
        tile_core(0)
        tile_end(qb, 0, qb + 1)
        tile_core(1)
        tile_end(qb + 1, 1, jnp.minimum(qb + 2, nt - 1))
        return carry

    lax.fori_loop(0, nq // 2, two_tiles, 0)
    if nq % 2:
        tile_core(0)
        softmax_pv(last, kvt, 0)
        finish(nt - 1)


def _attn_call(lam, g, qt, k, vt, *, mode, nct, lam_init, kvt=KV_TILES, unroll=PAIR_UNROLL):
    B, NT, GD, _ = qt.shape
    T = NT * TM
    G = vt.shape[2] // 128
    dk = GD // G
    body = functools.partial(_attn_body, mode=mode, nct=nct, nt=NT, lam_init=lam_init, kvt=kvt, unroll=unroll)
    return pl.pallas_call(
        body,
        grid=(B, G),
        in_specs=[
            _const_spec(lam.shape), _const_spec(g.shape),
            pl.BlockSpec((1, NT, dk, TM), lambda b, h: (b, 0, h, 0)),
            pl.BlockSpec((1, T, dk), lambda b, h: (b, 0, h)),
            pl.BlockSpec((1, NT, 128, TM), lambda b, h: (b, 0, h, 0)),
        ],
        out_specs=pl.BlockSpec((1, T, 128), lambda b, h: (b, 0, h)),
        out_shape=jax.ShapeDtypeStruct((B, T, G * 128), BF16),
        scratch_shapes=[
            pltpu.VMEM((1, 2 * TM), F32), pltpu.VMEM((1, 2 * TM), F32), pltpu.VMEM((128, 2 * TM), F32),
            pltpu.VMEM((2, kvt * TM, 2 * TM), F32), pltpu.VMEM((2, 1, 2 * TM), F32),
            pltpu.VMEM((dk, 2 * TM), BF16),
        ],
        compiler_params=_cparams(2),
        name="attn_" + mode,
    )(lam, g, qt, k, vt)


def _post_body(*refs, has_gate):
    if has_gate:
        x_ref, o_ref, gt_ref, mod_ref, wo_ref, n2_ref, w1_ref, w3_ref, w2_ref, out_ref = refs
        o = (o_ref[0] * gt_ref[0]).astype(BF16)
    else:
        x_ref, o_ref, mod_ref, wo_ref, n2_ref, w1_ref, w3_ref, w2_ref, out_ref = refs
        o = o_ref[0]
    m = mod_ref[0]
    x1 = x_ref[0] + m[2:3] * _dot(o, wo_ref[...])
    h = _norm_mod(x1, n2_ref[...], m[3:4], m[4:5]).astype(BF16)
    a = _dot(h, w1_ref[...])
    b = _dot(h, w3_ref[...])
    u = (a * jax.nn.sigmoid(a) * b).astype(BF16)
    out_ref[0] = x1 + m[5:6] * _dot(u, w2_ref[...])


def _post_call(xa, o, gate, mod_i, wo, n2, w1, w3, w2, nct):
    B, T, D = xa.shape
    F = w1.shape[1]
    x_spec, mod_spec = _tok_specs(B, nct, D)
    has_gate = gate is not None
    ins = [xa, o] + ([gate] if has_gate else []) + [mod_i, wo, n2, w1, w3, w2]
    one = pl.Buffered(1)
    specs = [x_spec, x_spec] + ([x_spec] if has_gate else []) + [
        mod_spec,
        pl.BlockSpec((D, D), lambda b, t: (0, 0), pipeline_mode=one),
        _const_spec((1, D)),
        pl.BlockSpec((D, F), lambda b, t: (0, 0), pipeline_mode=one),
        pl.BlockSpec((D, F), lambda b, t: (0, 0), pipeline_mode=one),
        pl.BlockSpec((F, D), lambda b, t: (0, 0), pipeline_mode=one),
    ]
    return pl.pallas_call(
        functools.partial(_post_body, has_gate=has_gate),
        grid=(B, T // TM),
        in_specs=specs,
        out_specs=x_spec,
        out_shape=jax.ShapeDtypeStruct((B, T, D), F32),
        input_output_aliases={0: 0},
        compiler_params=_cparams(2),
        name="post_ffn",
    )(*ins)


def _mla_pre_body(x_ref, mod_ref, g_ref, wext_ref, gq_ref, gkv_ref, wuqt_ref, wkn_ref, wvt_ref,
                  cq_ref, sq_ref, ck_ref, sk_ref, qt_ref, k_ref, vt_ref, *, scale):
    m = mod_ref[0]
    hb = _norm_mod(x_ref[0], g_ref[...], m[0:1], m[1:2]).astype(BF16)
    down = _dot(hb, wext_ref[...])
    ql = down[:, :MLA_Q_LORA]
    cq_lat = (ql * lax.rsqrt(jnp.mean(ql * ql, axis=-1, keepdims=True) + NORM_EPS) * gq_ref[...]).astype(BF16)
    kl = down[:, MLA_Q_LORA:MLA_Q_LORA + MLA_KV_LORA]
    ckv = (kl * lax.rsqrt(jnp.mean(kl * kl, axis=-1, keepdims=True) + NORM_EPS) * gkv_ref[...]).astype(BF16)
    kr = (down[:, 384:512] * ck_ref[...] + down[:, 512:640] * sk_ref[...]).astype(BF16)
    kn = _dot(ckv, wkn_ref[...])
    npair = kn.shape[1] // 128
    for p in range(npair):
        k_ref[0, :, 256 * p:256 * p + 128] = kn[:, 128 * p:128 * (p + 1)].astype(BF16)
        k_ref[0, :, 256 * p + 128:256 * (p + 1)] = kr
    vt_ref[0, 0] = _dot_nt(wvt_ref[...], ckv).astype(BF16)
    qt = _dot_nt(wuqt_ref[...], cq_lat)
    cq = cq_ref[...]
    sq = sq_ref[...]
    for p in range(npair):
        base = 256 * p
        qt_ref[0, 0, base:base + 128, :] = (qt[base:base + 128] * scale).astype(BF16)
        blk = [qt[base + 128 + 8 * i:base + 136 + 8 * i] for i in range(8)]
        for i in range(8):
            out = blk[i] * cq[8 * i:8 * i + 8] + blk[i ^ 1] * sq[8 * i:8 * i + 8]
            qt_ref[0, 0, base + 128 + 8 * i:base + 136 + 8 * i, :] = out.astype(BF16)
        qt_ref[0, 0, base + 192:base + 256, :] = jnp.zeros((64, qt.shape[1]), BF16)


def _mla_pre_call(xa, mod_i, g1, wext, gq, gkv, wuqt, wkn, wvt, cq, sq, ck, sk, nct, scale):
    B, T, D = xa.shape
    NT = T // TM
    x_spec, mod_spec = _tok_specs(B, nct, D)
    GD = wuqt.shape[0]
    return pl.pallas_call(
        functools.partial(_mla_pre_body, scale=scale),
        grid=(B, NT),
        in_specs=[
            x_spec, mod_spec, _const_spec((1, D)),
            _const_spec(wext.shape), _const_spec(gq.shape), _const_spec(gkv.shape),
            _const_spec(wuqt.shape), _const_spec(wkn.shape), _const_spec(wvt.shape),
            pl.BlockSpec((64, TM), lambda b, t: (0, t)), pl.BlockSpec((64, TM), lambda b, t: (0, t)),
            pl.BlockSpec((TM, 128), lambda b, t: (t, 0)), pl.BlockSpec((TM, 128), lambda b, t: (t, 0)),
        ],
        out_specs=[
            pl.BlockSpec((1, 1, GD, TM), lambda b, t: (b, t, 0, 0)),
            pl.BlockSpec((1, TM, GD), lambda b, t: (b, t, 0)),
            pl.BlockSpec((1, 1, D, TM), lambda b, t: (b, t, 0, 0)),
        ],
        out_shape=[
            jax.ShapeDtypeStruct((B, NT, GD, TM), BF16),
            jax.ShapeDtypeStruct((B, T, GD), BF16),
            jax.ShapeDtypeStruct((B, NT, D, TM), BF16),
        ],
        compiler_params=_cparams(2),
        name="mla_pre",
    )(xa, mod_i, g1, wext, gq, gkv, wuqt, wkn, wvt, cq, sq, ck, sk)


def _rw_pre_body(x_ref, xp_ref, xn_ref, mod_ref, g_ref, mu_ref, wr_ref, wk_ref, wv_ref, w1_ref, w2_ref, w0_ref,
                 a1_ref, a2_ref, a0_ref, g1_ref, g2_ref, r_ref, k_ref, v_ref, w_ref, a_ref, gt_ref, *, nct, nt):
    t = pl.program_id(1)
    m = mod_ref[0]
    g = g_ref[...]
    h = _norm_mod(x_ref[0], g, m[0:1], m[1:2])
    hp = _norm_mod(xp_ref[0], g, m[0:1], m[1:2])[7:8]
    hn = _norm_mod(xn_ref[0], g, m[0:1], m[1:2])[0:1]
    has_prev = jnp.logical_and(t != 0, t != nct)
    has_next = jnp.logical_and(t != nct - 1, t != nt - 1)
    hp = jnp.where(has_prev, hp, 0.0)
    hn = jnp.where(has_next, hn, 0.0)
    tm = h.shape[0]
    rid = lax.broadcasted_iota(jnp.int32, h.shape, 0)
    h_m1 = jnp.where(rid == 0, hp, pltpu.roll(h, 1, axis=0))
    h_p1 = jnp.where(rid == tm - 1, hn, pltpu.roll(h, tm - 1, axis=0))
    cs = 0.5 * (h_m1 + h_p1) - h
    mu = mu_ref[...]

    def mix(s):
        return (h + cs * mu[s:s + 1]).astype(BF16)

    r_ref[0] = _dot(mix(0), wr_ref[...])
    k_ref[0] = _dot(mix(1), wk_ref[...])
    v_ref[0] = _dot(mix(2), wv_ref[...])
    lw = jnp.tanh(_dot(mix(3), w1_ref[...])).astype(BF16)
    la = _dot(mix(4), a1_ref[...]).astype(BF16)
    for z in range(2):
        wl = w0_ref[z:z + 1] + _dot(lw, w2_ref[z])
        w_ref[z, 0] = jnp.exp(-math.exp(-0.5) * jax.nn.sigmoid(wl))
        a_ref[z, 0] = jax.nn.sigmoid(a0_ref[z:z + 1] + _dot(la, a2_ref[z]))
    gl = jax.nn.sigmoid(_dot(mix(5), g1_ref[...])).astype(BF16)
    gt_ref[0] = _dot(gl, g2_ref[...])


def _rw_pre_call(xa, mod_i, g1n, mu, wr, wk, wv, w1, w2, w0, a1, a2, a0, g1, g2, nct):
    B, T, D = xa.shape
    NT = T // TM
    x_spec, mod_spec = _tok_specs(B, nct, D)
    r8 = TM // 8
    prev_spec = pl.BlockSpec((1, 8, D), lambda b, t: (b, jnp.maximum(t * r8 - 1, 0), 0))
    next_spec = pl.BlockSpec((1, 8, D), lambda b, t: (b, jnp.minimum((t + 1) * r8, T // 8 - 1), 0))
    dir_spec = pl.BlockSpec((2, 1, TM, D), lambda b, t: (0, b, t, 0))
    consts = [g1n, mu, wr, wk, wv, w1, w2, w0, a1, a2, a0, g1, g2]
    tok = jax.ShapeDtypeStruct((B, T, D), F32)
    two = jax.ShapeDtypeStruct((2, B, T, D), F32)
    return pl.pallas_call(
        functools.partial(_rw_pre_body, nct=nct, nt=NT),
        grid=(B, NT),
        in_specs=[x_spec, prev_spec, next_spec, mod_spec] + [_const_spec(c.shape) for c in consts],
        out_specs=[x_spec, x_spec, x_spec, dir_spec, dir_spec, x_spec],
        out_shape=[tok, tok, tok, two, two, tok],
        compiler_params=_cparams(2),
        name="rw_pre",
    )(xa, xa, xa, mod_i, *consts)


def _rw_scan_body(r_ref, k_ref, v_ref, w_ref, a_ref, kk_ref, ka_ref, y_ref, s_ref, *, tc):
    d = pl.program_id(0)

    @pl.when(pl.program_id(1) == 0)
    def _():
        s_ref[...] = jnp.zeros(s_ref.shape, F32)

    nv, nk, bh = s_ref.shape
    rowid = lax.broadcasted_iota(jnp.int32, (8, bh), 0)
    low4, low2, low1 = (rowid & 4) == 0, (rowid & 2) == 0, (rowid & 1) == 0

    def fold(x, y, keep, dist):
        u = jnp.where(keep, x, y)
        w = jnp.where(keep, y, x)
        if dist == 4:
            return u + pltpu.roll(w, 4, axis=0)
        return u + jnp.where(keep, pltpu.roll(w, 8 - dist, axis=0), pltpu.roll(w, dist, axis=0))

    def row_sums(parts):
        z = [fold(parts[i], parts[i + 4], low4, 4) for i in range(4)]
        y = [fold(z[i], z[i + 2], low2, 2) for i in range(2)]
        return fold(y[0], y[1], low1, 1)

    def step(i, p):
        te = jnp.where(d == 0, i, tc - 1 - i)
        r = r_ref[te]
        k = k_ref[te]
        a = a_ref[te]
        kk = k * kk_ref[...]
        nrm = jnp.sqrt(jnp.sum(kk * kk, axis=0, keepdims=True))
        kk = kk / jnp.maximum(nrm, 1e-12)
        pn = p * w_ref[te]
        ip = 1.0 / pn
        at = -kk * p
        bt = kk * a * ip
        kt = k * (1.0 + (a - 1.0) * ka_ref[...]) * ip
        rt = r * pn
        for g in range(nv // 8):
            parts = []
            for j in range(8):
                vi = 8 * g + j
                s = s_ref[vi]
                sa = jnp.sum(s * at, axis=0, keepdims=True)
                vv = v_ref[te, pl.ds(vi, 1), :]
                sn = s + sa * bt + vv * kt
                s_ref[vi] = sn
                parts.append(jnp.sum((sn * rt).reshape(nk // 8, 8, bh), axis=0))
            y_ref[0, te, 8 * g:8 * g + 8, :] = row_sums(parts)
        return pn

    p_end = lax.fori_loop(0, tc, step, jnp.ones((nk, bh), F32))
    for vi in range(nv):
        s_ref[vi] = s_ref[vi] * p_end


def _rw_scan_call(r, k, v, w, a, kk_tab, ka_tab, ncc):
    T, K, BH = r.shape
    tc = SCAN_TC
    NC = T // tc

    def cidx(d, c):
        bwd = jnp.where(c < ncc, ncc - 1 - c, NC - 1 - (c - ncc))
        return jnp.where(d == 0, c, bwd)

    sh_spec = pl.BlockSpec((tc, K, BH), lambda d, c: (cidx(d, c), 0, 0))
    dir_spec = pl.BlockSpec((1, tc, K, BH), lambda d, c: (d, cidx(d, c), 0, 0))
    lane_spec = pl.BlockSpec((tc, K, BH), lambda d, c: (cidx(d, c), 0, d))
    return pl.pallas_call(
        functools.partial(_rw_scan_body, tc=tc),
        grid=(2, NC),
        in_specs=[sh_spec, sh_spec, sh_spec, lane_spec, lane_spec, _const_spec((K, BH)), _const_spec((K, BH))],
        out_specs=dir_spec,
        out_shape=jax.ShapeDtypeStruct((2, T, K, BH), F32),
        scratch_shapes=[pltpu.VMEM((K, K, BH), F32)],
        compiler_params=_cparams(2),
        name="rw_scan",
    )(r, k, v, w, a, kk_tab, ka_tab)


def _rw_fin_body(y_ref, r_ref, k_ref, v_ref, a0_ref, a1_ref, ka_ref, rk_ref, lg_ref, lb_ref, z_ref):
    y = y_ref[0] + y_ref[1]
    mean = jnp.mean(y, axis=1, keepdims=True)
    yc = y - mean
    var = jnp.mean(yc * yc, axis=1, keepdims=True)
    yn = yc * lax.rsqrt(var + RW_GN_EPS) * lg_ref[...] + lb_ref[...]
    k = k_ref[...]
    ka = ka_ref[...]
    ksum = k * (1.0 + (a0_ref[...] - 1.0) * ka) + k * (1.0 + (a1_ref[...] - 1.0) * ka)
    bonus = jnp.sum(r_ref[...] * ksum * rk_ref[...], axis=1, keepdims=True) * v_ref[...]
    z_ref[...] = yn + bonus


def _rw_fin_call(y, r, k, v, a, ka_tab, rk_tab, lg_tab, lb_tab):
    T, K, BH = r.shape
    tc = SCAN_TC
    sh_spec = pl.BlockSpec((tc, K, BH), lambda c: (c, 0, 0))
    dir_spec = pl.BlockSpec((2, tc, K, BH), lambda c: (0, c, 0, 0))
    tab = _const_spec((K, BH))
    return pl.pallas_call(
        _rw_fin_body,
        grid=(T // tc,),
        in_specs=[dir_spec, sh_spec, sh_spec, sh_spec, sh_spec,
                  pl.BlockSpec((tc, K, BH), lambda c: (c, 0, 1)), tab, tab, tab, tab],
        out_specs=sh_spec,
        out_shape=jax.ShapeDtypeStruct((T, K, BH), F32),
        compiler_params=_cparams(1),
        name="rw_fin",
    )(y, r, k, v, a, a, ka_tab, rk_tab, lg_tab, lb_tab)


def _final_body(x_ref, g_ref, o_ref):
    x = x_ref[0]
    ms = jnp.mean(x * x, axis=-1, keepdims=True)
    o_ref[0] = x * lax.rsqrt(ms + NORM_EPS) * g_ref[...]


def _final_call(xa, g, nct):
    B, T, D = xa.shape
    N = T - nct * TM
    return pl.pallas_call(
        _final_body,
        grid=(B, N // TM),
        in_specs=[pl.BlockSpec((1, TM, D), lambda b, t: (b, t + nct, 0)), _const_spec((1, D))],
        out_specs=pl.BlockSpec((1, TM, D), lambda b, t: (b, t, 0)),
        out_shape=jax.ShapeDtypeStruct((B, N, D), F32),
        compiler_params=_cparams(2),
        name="final_norm",
    )(xa, g)


def _rope_tables(n_ctx, n_lat, rot_dim):
    mfreq = rot_dim // 4
    t = jnp.arange(n_lat)
    row = (t // GRID_W).astype(F32)
    col = (t % GRID_W).astype(F32)
    inv = ROPE_BASE ** (-jnp.arange(mfreq, dtype=F32) / mfreq)
    ar = row[:, None] * inv
    ac = col[:, None] * inv
    cos = jnp.concatenate([jnp.cos(ar), jnp.cos(ar), jnp.cos(ac), jnp.cos(ac)], axis=1)
    sin = jnp.concatenate([-jnp.sin(ar), jnp.sin(ar), -jnp.sin(ac), jnp.sin(ac)], axis=1)
    cos = jnp.concatenate([jnp.ones((n_ctx, rot_dim), F32), cos], axis=0)
    sin = jnp.concatenate([jnp.zeros((n_ctx, rot_dim), F32), sin], axis=0)
    return cos, sin


def _swap_perm(n, half):
    idx = jnp.arange(n)
    return idx ^ half


def _to_scan(x, B):
    lead = x.shape[:-3]
    T, D = x.shape[-2:]
    H = D // RW_HEAD
    x = x.reshape(lead + (B, T, H, RW_HEAD))
    nl = len(lead)
    perm = tuple(range(nl)) + (nl + 1, nl + 3, nl, nl + 2)
    return jnp.transpose(x, perm).reshape(lead + (T, RW_HEAD, B * H))


def _head_tab(p, B):
    H = p.size // RW_HEAD
    t = p.reshape(H, RW_HEAD).T
    return jnp.tile(t[:, None, :], (1, B, 1)).reshape(RW_HEAD, B * H).astype(F32)


def kernel(x, c, ctx, c_ctx, ada_w, ada_b, norm1_g, norm2_g, ffn_w1, ffn_w3, ffn_w2, final_norm_g, da_wqkv, da_lambda, da_subln_g, da_wo, rw_mu, rw_wrkv, rw_w0, rw_w1, rw_w2, rw_a0, rw_a1, rw_a2, rw_g1, rw_g2, rw_k_k, rw_k_a, rw_r_k, rw_lnx_g, rw_lnx_b, rw_wo, mla_wdown, mla_q_norm_g, mla_wuq, mla_kv_norm_g, mla_wukv, mla_wo):
    B, N, D = x.shape
    n_ctx = ctx.shape[1]
    depth = ada_w.shape[0]
    assert n_ctx % TM == 0 and N % TM == 0 and n_ctx % SCAN_TC == 0
    nct = n_ctx // TM
    T = n_ctx + N
    assert (T // TM) % KV_TILES == 0 and (T // TM // KV_TILES) % 2 == 1

    xa = jnp.concatenate([ctx, x], axis=1)
    R = -(-(B + 1) // 8) * 8
    cond = jnp.zeros((R, D), F32).at[:B].set(c).at[B].set(c_ctx)
    mod = _ada_call(cond, ada_w, ada_b).reshape(depth, R, 6, D)

    da_cos, da_sin = _rope_tables(n_ctx, N, DA_HEAD_DIM)
    da_scale = (DA_HEAD_DIM ** -0.5) * LOG2E
    da_cq, da_sq = (da_cos * da_scale).T, (da_sin * da_scale).T
    da_ck, da_sk = jnp.tile(da_cos, (1, 2)), jnp.tile(da_sin, (1, 2))
    ml_cos, ml_sin = _rope_tables(n_ctx, N, MLA_ROPE)
    ml_scale = ((MLA_NOPE + MLA_ROPE) ** -0.5) * LOG2E
    ml_cq, ml_sq = jnp.tile((ml_cos * ml_scale).T, (2, 1)), jnp.tile((ml_sin * ml_scale).T, (2, 1))
    zpad = jnp.zeros((T, 64), F32)
    ml_ck = jnp.concatenate([ml_cos, ml_cos, zpad], axis=1)
    ml_sk = jnp.concatenate([ml_sin, ml_sin, zpad], axis=1)

    for i in range(depth):
        kind, j = i % N_MIXERS, i // N_MIXERS
        mod_i = mod[i]
        g1 = norm1_g[i].reshape(1, D)
        gate = None
        if kind == 0:
            wq, wk, wv = jnp.split(da_wqkv[j], 3, axis=1)
            perm = _swap_perm(D, 16)
            qt, kk, vt = _da_pre_call(xa, mod_i, g1, wq.T.astype(BF16), wk.astype(BF16),
                                      wv.T.astype(BF16), da_cq, da_sq, da_ck, da_sk, nct)
            lam_init = 0.8 - 0.6 * math.exp(-0.3 * i)
            o = _attn_call(da_lambda[j], da_subln_g[j].reshape(1, -1), qt, kk, vt,
                           mode="diff", nct=nct, lam_init=lam_init)
            wo = da_wo[j]
        elif kind == 1:
            H = D // RW_HEAD
            w1c = jnp.concatenate([rw_w1[j, 0], rw_w1[j, 1]], axis=1).astype(BF16)
            a1c = jnp.concatenate([rw_a1[j, 0], rw_a1[j, 1]], axis=1).astype(BF16)
            zl = jnp.zeros_like(rw_w2[j, 0])
            w2p = jnp.stack([jnp.concatenate([rw_w2[j, 0], zl], axis=0),
                             jnp.concatenate([zl, rw_w2[j, 1]], axis=0)]).astype(BF16)
            a2p = jnp.stack([jnp.concatenate([rw_a2[j, 0], zl], axis=0),
                             jnp.concatenate([zl, rw_a2[j, 1]], axis=0)]).astype(BF16)
            gl = rw_g1.shape[2]
            glp = -(-gl // 128) * 128
            g1p = jnp.pad(rw_g1[j], ((0, 0), (0, glp - gl))).astype(BF16)
            g2p = jnp.pad(rw_g2[j], ((0, glp - gl), (0, 0))).astype(BF16)
            r, k, v, w, a, gate = _rw_pre_call(
                xa, mod_i, g1, rw_mu[j], rw_wrkv[j, 0].astype(BF16), rw_wrkv[j, 1].astype(BF16),
                rw_wrkv[j, 2].astype(BF16), w1c, w2p, rw_w0[j], a1c, a2p, rw_a0[j], g1p, g2p, nct)
            rs, ks, vs = (_to_scan(u, B) for u in (r, k, v))
            ws, as_ = (_to_scan(u.reshape(2 * B, T, D), 2 * B) for u in (w, a))
            ka_tab = _head_tab(rw_k_a[j], B)
            y = _rw_scan_call(rs, ks, vs, ws, as_, _head_tab(rw_k_k[j], B), ka_tab, n_ctx // SCAN_TC)
            z = _rw_fin_call(y, rs, ks, vs, as_, ka_tab, _head_tab(rw_r_k[j], B),
                             _head_tab(rw_lnx_g[j], B), _head_tab(rw_lnx_b[j], B))
            o = jnp.transpose(z.reshape(T, RW_HEAD, B, H), (2, 0, 3, 1)).reshape(B, T, D)
            wo = rw_wo[j]
        else:
            wd = mla_wdown[j]
            nlat = MLA_Q_LORA + MLA_KV_LORA
            kr = wd[:, nlat:]
            krs = kr[:, _swap_perm(MLA_ROPE, 8)]
            zc = jnp.zeros((D, 64), F32)
            wext = jnp.concatenate([wd[:, :nlat], kr, kr, zc, krs, krs, zc], axis=1).astype(BF16)
            wuq = mla_wuq[j].reshape(MLA_Q_LORA, MLA_HEADS // 2, 2, MLA_NOPE + MLA_ROPE)
            qn = wuq[..., :MLA_NOPE].reshape(MLA_Q_LORA, MLA_HEADS // 2, 2 * MLA_NOPE)
            qr = wuq[..., MLA_NOPE:].reshape(MLA_Q_LORA, MLA_HEADS // 2, 2 * MLA_ROPE)
            qz = jnp.zeros((MLA_Q_LORA, MLA_HEADS // 2, 64), F32)
            wuqt = jnp.concatenate([qn, qr, qz], axis=2).reshape(MLA_Q_LORA, -1).T.astype(BF16)
            wukv = mla_wukv[j].reshape(MLA_KV_LORA, MLA_HEADS, MLA_NOPE + MLA_V)
            wkn = wukv[..., :MLA_NOPE].reshape(MLA_KV_LORA, -1).astype(BF16)
            wvt = wukv[..., MLA_NOPE:].reshape(MLA_KV_LORA, -1).T.astype(BF16)
            qt, kk, vt = _mla_pre_call(xa, mod_i, g1, wext, mla_q_norm_g[j].reshape(1, -1),
                                       mla_kv_norm_g[j].reshape(1, -1), wuqt, wkn, wvt,
                                       ml_cq, ml_sq, ml_ck, ml_sk, nct, ml_scale)
            o = _attn_call(jnp.zeros((4, 64), F32), jnp.ones((1, 128), F32), qt, kk, vt,
                           mode="mla", nct=nct, lam_init=0.0)
            wo = mla_wo[j]
        xa = _post_call(xa, o, gate, mod_i, wo.astype(BF16), norm2_g[i].reshape(1, D),
                        ffn_w1[i].astype(BF16), ffn_w3[i].astype(BF16), ffn_w2[i].astype(BF16), nct)
    return _final_call(xa, final_norm_g.reshape(1, D), nct)
```

```python
import functools
import math

import jax
import jax.numpy as jnp
from jax import lax
from jax.experimental import pallas as pl
from jax.experimental.pallas import tpu as pltpu

F32 = jnp.float32
BF16 = jnp.bfloat16

GRID_W = 64
ROPE_BASE = 10000.0
NORM_EPS = 1e-6
N_MIXERS = 3
DA_HEAD_DIM = 64
DA_SUBLN_EPS = 1e-5
RW_HEAD = 64
RW_GN_EPS = 64e-5
MLA_HEADS = 16
MLA_Q_LORA = 256
MLA_KV_LORA = 128
MLA_NOPE = 64
MLA_ROPE = 32
MLA_V = 64
LOG2E = math.log2(math.e)

TM = 256
KV_TILES = 3
PAIR_UNROLL = 5
SCAN_TC = 32
ADA_TN = 1536
VMEM_LIMIT = 56 * 1024 * 1024


def _cparams(n_grid):
    return pltpu.CompilerParams(dimension_semantics=("arbitrary",) * n_grid, vmem_limit_bytes=VMEM_LIMIT)


def _const_spec(shape):
    nd = len(shape)
    return pl.BlockSpec(shape, lambda *_: (0,) * nd)


def _norm_mod(x, g, shift, scale):
    ms = jnp.mean(x * x, axis=-1, keepdims=True)
    return (x * lax.rsqrt(ms + NORM_EPS) * g) * (1.0 + scale) + shift


def _dot(a, b):
    return jnp.dot(a, b, preferred_element_type=F32)


def _dot_nt(a, b):
    return lax.dot_general(a, b, (((1,), (1,)), ((), ())), preferred_element_type=F32)


def _ada_body(c_ref, w_ref, b_ref, o_ref):
    c = c_ref[...]
    s = (c * jax.nn.sigmoid(c)).astype(BF16)
    o_ref[0] = _dot(s, w_ref[0].astype(BF16)) + b_ref[0]


def _ada_call(cond, ada_w, ada_b):
    L, D, D6 = ada_w.shape
    R = cond.shape[0]
    return pl.pallas_call(
        _ada_body,
        grid=(L, D6 // ADA_TN),
        in_specs=[
            pl.BlockSpec((R, D), lambda l, j: (0, 0)),
            pl.BlockSpec((1, D, ADA_TN), lambda l, j: (l, 0, j)),
            pl.BlockSpec((1, 1, ADA_TN), lambda l, j: (l, 0, j)),
        ],
        out_specs=pl.BlockSpec((1, R, ADA_TN), lambda l, j: (l, 0, j)),
        out_shape=jax.ShapeDtypeStruct((L, R, D6), F32),
        compiler_params=_cparams(2),
        name="ada_mod",
    )(cond, ada_w, ada_b.reshape(L, 1, D6))


def _tok_specs(B, nct, D):
    x_spec = pl.BlockSpec((1, TM, D), lambda b, t: (b, t, 0))
    mod_spec = pl.BlockSpec((1, 6, D), lambda b, t: (jnp.where(t < nct, B, b), 0, 0))
    return x_spec, mod_spec


def _da_pre_body(x_ref, mod_ref, g_ref, wqt_ref, wk_ref, wvt_ref, cq_ref, sq_ref, ck_ref, sk_ref,
                 qt_ref, k_ref, vt_ref):
    m = mod_ref[0]
    hb = _norm_mod(x_ref[0], g_ref[...], m[0:1], m[1:2]).astype(BF16)
    D = hb.shape[1]
    qt = _dot_nt(wqt_ref[...], hb)
    cq = cq_ref[...]
    sq = sq_ref[...]
    hd = DA_HEAD_DIM
    for h in range(D // hd):
        blk = [qt[h * hd + 16 * i:h * hd + 16 * (i + 1)] for i in range(4)]
        for i in range(4):
            r0 = 16 * i
            out = blk[i] * cq[r0:r0 + 16] + blk[i ^ 1] * sq[r0:r0 + 16]
            qt_ref[0, 0, h * hd + r0:h * hd + r0 + 16, :] = out.astype(BF16)
    k = _dot(hb, wk_ref[...])
    ck = ck_ref[...]
    sk = sk_ref[...]
    lane = lax.broadcasted_iota(jnp.int32, ck.shape, 1)
    first_half = (lane & 16) == 0
    for j in range(D // 128):
        kj = k[:, 128 * j:128 * (j + 1)]
        ks = jnp.where(first_half, pltpu.roll(kj, 112, axis=1), pltpu.roll(kj, 16, axis=1))
        k_ref[0, :, 128 * j:128 * (j + 1)] = (kj * ck + ks * sk).astype(BF16)
    vt_ref[0, 0] = _dot_nt(wvt_ref[...], hb).astype(BF16)


def _da_pre_call(xa, mod_i, g1, wqt, wk, wvt, cq, sq, ck, sk, nct):
    B, T, D = xa.shape
    NT = T // TM
    x_spec, mod_spec = _tok_specs(B, nct, D)
    fm_spec = pl.BlockSpec((1, 1, D, TM), lambda b, t: (b, t, 0, 0))
    return pl.pallas_call(
        _da_pre_body,
        grid=(B, NT),
        in_specs=[
            x_spec, mod_spec, _const_spec((1, D)),
            _const_spec((D, D)), _const_spec((D, D)), _const_spec((D, D)),
            pl.BlockSpec((64, TM), lambda b, t: (0, t)), pl.BlockSpec((64, TM), lambda b, t: (0, t)),
            pl.BlockSpec((TM, 128), lambda b, t: (t, 0)), pl.BlockSpec((TM, 128), lambda b, t: (t, 0)),
        ],
        out_specs=[fm_spec, x_spec, fm_spec],
        out_shape=[
            jax.ShapeDtypeStruct((B, NT, D, TM), BF16),
            jax.ShapeDtypeStruct((B, T, D), BF16),
            jax.ShapeDtypeStruct((B, NT, D, TM), BF16),
        ],
        compiler_params=_cparams(2),
        name="da_pre",
    )(xa, mod_i, g1, wqt, wk, wvt, cq, sq, ck, sk)


def _attn_body(lam_ref, g_ref, q_ref, k_ref, v_ref, o_ref, m_ref, l_ref, acc_ref, s_ref, mx_ref, q2_ref,
               *, mode, nct, nt, lam_init, kvt, unroll):
    dk, tq = q_ref.shape[2], q_ref.shape[3]

    def prep_q(qb):
        q = q_ref[0, qb].astype(F32)
        rows = lax.broadcasted_iota(jnp.int32, (dk, tq), 0)
        if mode == "diff":
            sel_a = rows < 64
            qa = jnp.where(sel_a, q, 0.0)
            qb_ = jnp.where(sel_a, 0.0, q)
        else:
            blk = rows // 32
            in_a = (blk == 0) | (blk == 1) | (blk == 4)
            in_b = (blk == 2) | (blk == 3) | (blk == 5)
            qa = jnp.where(in_a, q, 0.0)
            qb_ = jnp.where(in_b, q, 0.0)
        q2_ref[...] = jnp.concatenate([qa, qb_], axis=1).astype(BF16)

    def init():
        m_ref[...] = jnp.full(m_ref.shape, -1e30, F32)
        l_ref[...] = jnp.zeros(l_ref.shape, F32)
        acc_ref[...] = jnp.zeros(acc_ref.shape, F32)

    def qk(c0, ntile, slot):
        tk = ntile * TM
        k0 = c0 * TM if isinstance(c0, int) else pl.multiple_of(c0 * TM, TM)
        s = _dot(k_ref[0, pl.ds(k0, tk), :], q2_ref[...])
        s_ref[slot, 0:tk, 0:2 * tq] = s
        mx_ref[slot] = jnp.max(s, axis=0, keepdims=True)

    def softmax_pv(c0, ntile, slot):
        s = s_ref[slot, 0:ntile * TM, 0:2 * tq]
        m_prev = m_ref[...]
        m_new = jnp.maximum(m_prev, mx_ref[slot])
        alpha = jnp.exp2(m_prev - m_new)
        p = jnp.exp2(s - m_new)
        l_ref[...] = alpha * l_ref[...] + jnp.sum(p, axis=0, keepdims=True)
        pb = p.astype(BF16)
        vc = jnp.concatenate([v_ref[0, c0 + j] for j in range(ntile)], axis=1)
        acc_ref[...] = alpha * acc_ref[...] + _dot(vc, pb)
        m_ref[...] = m_new

    def finish(qb):
        a = acc_ref[...] * (1.0 / l_ref[...])
        if mode == "diff":
            lv = lam_ref[...]
            lam = (jnp.exp(jnp.sum(lv[0:1] * lv[1:2], keepdims=True))
                   - jnp.exp(jnp.sum(lv[2:3] * lv[3:4], keepdims=True)) + lam_init)
            o = a[:, :tq] - lam * a[:, tq:]
            ms = jnp.mean(o * o, axis=0, keepdims=True)
            o = o * lax.rsqrt(ms + DA_SUBLN_EPS)
            ot = (o.T * g_ref[...]) * (1.0 - lam_init)
        else:
            ot = jnp.concatenate([a[:64, :tq], a[64:, tq:]], axis=0).T
        if isinstance(qb, int):
            o_ref[0, qb * TM:(qb + 1) * TM, :] = ot.astype(BF16)
        else:
            o_ref[0, pl.ds(pl.multiple_of(qb * TM, TM), TM), :] = ot.astype(BF16)

    for qb in range(nct):
        prep_q(qb)
        init()
        for c in range(nct):
            qk(c, 1, 0)
            softmax_pv(c, 1, 0)
        finish(qb)

    n = nt // kvt
    nq = nt - nct
    last = (n - 1) * kvt
    npairs = (n - 1) // 2
    if npairs % unroll:
        unroll = 1

    def tile_core(slot0):
        init()

        def pairs(j, carry):
            for u in range(unroll):
                c = 2 * (j * unroll + u) * kvt
                qk(c + kvt, kvt, 1 - slot0)
                softmax_pv(c, kvt, slot0)
                qk(c + 2 * kvt, kvt, slot0)
                softmax_pv(c + kvt, kvt, 1 - slot0)
            return carry

        lax.fori_loop(0, npairs // unroll + jnp.minimum(pl.program_id(0), 0), pairs, 0)

    def tile_end(qb, slot0, qb_next):
        prep_q(qb_next)
        qk(0, kvt, 1 - slot0)
        softmax_pv(last, kvt, slot0)
        finish(qb)

    prep_q(nct)
    qk(0, kvt, 0)

    def two_tiles(i, carry):
        qb = nct + 2 * i
        tile_core(0)
        tile_end(qb, 0, qb + 1)
        tile_core(1)
        tile_end(qb + 1, 1, jnp.minimum(qb + 2, nt - 1))
        return carry

    lax.fori_loop(0, nq // 2, two_tiles, 0)
    if nq % 2:
        tile_core(0)
        softmax_pv(last, kvt, 0)
        finish(nt - 1)


def _attn_call(lam, g, qt, k, vt, *, mode, nct, lam_init, kvt=KV_TILES, unroll=PAIR_UNROLL):
    B, NT, GD, _ = qt.shape
    T = NT * TM
    G = vt.shape[2] // 128
    dk = GD // G
    body = functools.partial(_attn_body, mode=mode, nct=nct, nt=NT, lam_init=lam_init, kvt=kvt, unroll=unroll)
    return pl.pallas_call(
        body,
        grid=(B, G),
        in_specs=[
            _const_spec(lam.shape), _const_spec(g.shape),
            pl.BlockSpec((1, NT, dk, TM), lambda b, h: (b, 0, h, 0)),
            pl.BlockSpec((1, T, dk), lambda b, h: (b, 0, h)),
            pl.BlockSpec((1, NT, 128, TM), lambda b, h: (b, 0, h, 0)),
        ],
        out_specs=pl.BlockSpec((1, T, 128), lambda b, h: (b, 0, h)),
        out_shape=jax.ShapeDtypeStruct((B, T, G * 128), BF16),
        scratch_shapes=[
            pltpu.VMEM((1, 2 * TM), F32), pltpu.VMEM((1, 2 * TM), F32), pltpu.VMEM((128, 2 * TM), F32),
            pltpu.VMEM((2, kvt * TM, 2 * TM), F32), pltpu.VMEM((2, 1, 2 * TM), F32),
            pltpu.VMEM((dk, 2 * TM), BF16),
        ],
        compiler_params=_cparams(2),
        name="attn_" + mode,
    )(lam, g, qt, k, vt)


def _post_body(*refs, has_gate):
    if has_gate:
        x_ref, o_ref, gt_ref, mod_ref, wo_ref, n2_ref, w1_ref, w3_ref, w2_ref, out_ref = refs
        o = (o_ref[0] * gt_ref[0]).astype(BF16)
    else:
        x_ref, o_ref, mod_ref, wo_ref, n2_ref, w1_ref, w3_ref, w2_ref, out_ref = refs
        o = o_ref[0]
    m = mod_ref[0]
    x1 = x_ref[0] + m[2:3] * _dot(o, wo_ref[...])
    h = _norm_mod(x1, n2_ref[...], m[3:4], m[4:5]).astype(BF16)
    a = _dot(h, w1_ref[...])
    b = _dot(h, w3_ref[...])
    u = (a * jax.nn.sigmoid(a) * b).astype(BF16)
    out_ref[0] = x1 + m[5:6] * _dot(u, w2_ref[...])


def _post_call(xa, o, gate, mod_i, wo, n2, w1, w3, w2, nct):
    B, T, D = xa.shape
    F = w1.shape[1]
    x_spec, mod_spec = _tok_specs(B, nct, D)
    has_gate = gate is not None
    ins = [xa, o] + ([gate] if has_gate else []) + [mod_i, wo, n2, w1, w3, w2]
    one = pl.Buffered(1)
    specs = [x_spec, x_spec] + ([x_spec] if has_gate else []) + [
        mod_spec,
        pl.BlockSpec((D, D), lambda b, t: (0, 0), pipeline_mode=one),
        _const_spec((1, D)),
        pl.BlockSpec((D, F), lambda b, t: (0, 0), pipeline_mode=one),
        pl.BlockSpec((D, F), lambda b, t: (0, 0), pipeline_mode=one),
        pl.BlockSpec((F, D), lambda b, t: (0, 0), pipeline_mode=one),
    ]
    return pl.pallas_call(
        functools.partial(_post_body, has_gate=has_gate),
        grid=(B, T // TM),
        in_specs=specs,
        out_specs=x_spec,
        out_shape=jax.ShapeDtypeStruct((B, T, D), F32),
        input_output_aliases={0: 0},
        compiler_params=_cparams(2),
        name="post_ffn",
    )(*ins)


def _mla_pre_body(x_ref, mod_ref, g_ref, wext_ref, gq_ref, gkv_ref, wuqt_ref, wkn_ref, wvt_ref,
                  cq_ref, sq_ref, ck_ref, sk_ref, qt_ref, k_ref, vt_ref, *, scale):
    m = mod_ref[0]
    hb = _norm_mod(x_ref[0], g_ref[...], m[0:1], m[1:2]).astype(BF16)
    down = _dot(hb, wext_ref[...])
    ql = down[:, :MLA_Q_LORA]
    cq_lat = (ql * lax.rsqrt(jnp.mean(ql * ql, axis=-1, keepdims=True) + NORM_EPS) * gq_ref[...]).astype(BF16)
    kl = down[:, MLA_Q_LORA:MLA_Q_LORA + MLA_KV_LORA]
    ckv = (kl * lax.rsqrt(jnp.mean(kl * kl, axis=-1, keepdims=True) + NORM_EPS) * gkv_ref[...]).astype(BF16)
    kr = (down[:, 384:512] * ck_ref[...] + down[:, 512:640] * sk_ref[...]).astype(BF16)
    kn = _dot(ckv, wkn_ref[...])
    npair = kn.shape[1] // 128
    for p in range(npair):
        k_ref[0, :, 256 * p:256 * p + 128] = kn[:, 128 * p:128 * (p + 1)].astype(BF16)
        k_ref[0, :, 256 * p + 128:256 * (p + 1)] = kr
    vt_ref[0, 0] = _dot_nt(wvt_ref[...], ckv).astype(BF16)
    qt = _dot_nt(wuqt_ref[...], cq_lat)
    cq = cq_ref[...]
    sq = sq_ref[...]
    for p in range(npair):
        base = 256 * p
        qt_ref[0, 0, base:base + 128, :] = (qt[base:base + 128] * scale).astype(BF16)
        blk = [qt[base + 128 + 8 * i:base + 136 + 8 * i] for i in range(8)]
        for i in range(8):
            out = blk[i] * cq[8 * i:8 * i + 8] + blk[i ^ 1] * sq[8 * i:8 * i + 8]
            qt_ref[0, 0, base + 128 + 8 * i:base + 136 + 8 * i, :] = out.astype(BF16)
        qt_ref[0, 0, base + 192:base + 256, :] = jnp.zeros((64, qt.shape[1]), BF16)


def _mla_pre_call(xa, mod_i, g1, wext, gq, gkv, wuqt, wkn, wvt, cq, sq, ck, sk, nct, scale):
    B, T, D = xa.shape
    NT = T // TM
    x_spec, mod_spec = _tok_specs(B, nct, D)
    GD = wuqt.shape[0]
    return pl.pallas_call(
        functools.partial(_mla_pre_body, scale=scale),
        grid=(B, NT),
        in_specs=[
            x_spec, mod_spec, _const_spec((1, D)),
            _const_spec(wext.shape), _const_spec(gq.shape), _const_spec(gkv.shape),
            _const_spec(wuqt.shape), _const_spec(wkn.shape), _const_spec(wvt.shape),
            pl.BlockSpec((64, TM), lambda b, t: (0, t)), pl.BlockSpec((64, TM), lambda b, t: (0, t)),
            pl.BlockSpec((TM, 128), lambda b, t: (t, 0)), pl.BlockSpec((TM, 128), lambda b, t: (t, 0)),
        ],
        out_specs=[
            pl.BlockSpec((1, 1, GD, TM), lambda b, t: (b, t, 0, 0)),
            pl.BlockSpec((1, TM, GD), lambda b, t: (b, t, 0)),
            pl.BlockSpec((1, 1, D, TM), lambda b, t: (b, t, 0, 0)),
        ],
        out_shape=[
            jax.ShapeDtypeStruct((B, NT, GD, TM), BF16),
            jax.ShapeDtypeStruct((B, T, GD), BF16),
            jax.ShapeDtypeStruct((B, NT, D, TM), BF16),
        ],
        compiler_params=_cparams(2),
        name="mla_pre",
    )(xa, mod_i, g1, wext, gq, gkv, wuqt, wkn, wvt, cq, sq, ck, sk)


def _rw_pre_body(x_ref, xp_ref, xn_ref, mod_ref, g_ref, mu_ref, wr_ref, wk_ref, wv_ref, w1_ref, w2_ref, w0_ref,
                 a1_ref, a2_ref, a0_ref, g1_ref, g2_ref, r_ref, k_ref, v_ref, wf_ref, wb_ref, af_ref, ab_ref, gt_ref,
                 *, nct, nt):
    t = pl.program_id(1)
    m = mod_ref[0]
    g = g_ref[...]
    h = _norm_mod(x_ref[0], g, m[0:1], m[1:2])
    hp = _norm_mod(xp_ref[0], g, m[0:1], m[1:2])[7:8]
    hn = _norm_mod(xn_ref[0], g, m[0:1], m[1:2])[0:1]
    has_prev = jnp.logical_and(t != 0, t != nct)
    has_next = jnp.logical_and(t != nct - 1, t != nt - 1)
    hp = jnp.where(has_prev, hp, 0.0)
    hn = jnp.where(has_next, hn, 0.0)
    tm = h.shape[0]
    rid = lax.broadcasted_iota(jnp.int32, h.shape, 0)
    h_m1 = jnp.where(rid == 0, hp, pltpu.roll(h, 1, axis=0))
    h_p1 = jnp.where(rid == tm - 1, hn, pltpu.roll(h, tm - 1, axis=0))
    cs = 0.5 * (h_m1 + h_p1) - h
    mu = mu_ref[...]

    def mix(s):
        return (h + cs * mu[s:s + 1]).astype(BF16)

    r_ref[0] = _dot(mix(0), wr_ref[...])
    k_ref[0] = _dot(mix(1), wk_ref[...])
    v_ref[0] = _dot(mix(2), wv_ref[...])
    lw = jnp.tanh(_dot(mix(3), w1_ref[...])).astype(BF16)
    la = _dot(mix(4), a1_ref[...]).astype(BF16)
    for z in range(2):
        wl = w0_ref[z:z + 1] + _dot(lw, w2_ref[z])
        (wf_ref, wb_ref)[z][0] = jnp.exp(-math.exp(-0.5) * jax.nn.sigmoid(wl))
        (af_ref, ab_ref)[z][0] = jax.nn.sigmoid(a0_ref[z:z + 1] + _dot(la, a2_ref[z]))
    gl = jax.nn.sigmoid(_dot(mix(5), g1_ref[...])).astype(BF16)
    gt_ref[0] = _dot(gl, g2_ref[...])


def _rw_pre_call(xa, mod_i, g1n, mu, wr, wk, wv, w1, w2, w0, a1, a2, a0, g1, g2, nct):
    B, T, D = xa.shape
    NT = T // TM
    x_spec, mod_spec = _tok_specs(B, nct, D)
    r8 = TM // 8
    prev_spec = pl.BlockSpec((1, 8, D), lambda b, t: (b, jnp.maximum(t * r8 - 1, 0), 0))
    next_spec = pl.BlockSpec((1, 8, D), lambda b, t: (b, jnp.minimum((t + 1) * r8, T // 8 - 1), 0))
    consts = [g1n, mu, wr, wk, wv, w1, w2, w0, a1, a2, a0, g1, g2]
    tok = jax.ShapeDtypeStruct((B, T, D), F32)
    return pl.pallas_call(
        functools.partial(_rw_pre_body, nct=nct, nt=NT),
        grid=(B, NT),
        in_specs=[x_spec, prev_spec, next_spec, mod_spec] + [_const_spec(c.shape) for c in consts],
        out_specs=[x_spec] * 8,
        out_shape=[tok] * 8,
        compiler_params=_cparams(2),
        name="rw_pre",
    )(xa, xa, xa, mod_i, *consts)


def _rw_scan_body(r_ref, k_ref, v_ref, w_ref, a_ref, kk_ref, ka_ref, y_ref, s_ref, *, tc, reverse):
    @pl.when(pl.program_id(0) == 0)
    def _():
        s_ref[...] = jnp.zeros(s_ref.shape, F32)

    nv, nk, bh = s_ref.shape
    rowid = lax.broadcasted_iota(jnp.int32, (8, bh), 0)
    low4, low2, low1 = (rowid & 4) == 0, (rowid & 2) == 0, (rowid & 1) == 0

    def fold(x, y, keep, dist):
        u = jnp.where(keep, x, y)
        w = jnp.where(keep, y, x)
        if dist == 4:
            return u + pltpu.roll(w, 4, axis=0)
        return u + jnp.where(keep, pltpu.roll(w, 8 - dist, axis=0), pltpu.roll(w, dist, axis=0))

    def row_sums(parts):
        z = [fold(parts[i], parts[i + 4], low4, 4) for i in range(4)]
        y = [fold(z[i], z[i + 2], low2, 2) for i in range(2)]
        return fold(y[0], y[1], low1, 1)

    def step(i, p):
        te = tc - 1 - i if reverse else i
        r = r_ref[te]
        k = k_ref[te]
        a = a_ref[te]
        kk = k * kk_ref[...]
        nrm = jnp.sqrt(jnp.sum(kk * kk, axis=0, keepdims=True))
        kk = kk / jnp.maximum(nrm, 1e-12)
        pn = p * w_ref[te]
        ip = 1.0 / pn
        at = -kk * p
        bt = kk * a * ip
        kt = k * (1.0 + (a - 1.0) * ka_ref[...]) * ip
        rt = r * pn
        for g in range(nv // 8):
            parts = []
            for j in range(8):
                vi = 8 * g + j
                s = s_ref[vi]
                sa = jnp.sum(s * at, axis=0, keepdims=True)
                vv = v_ref[te, pl.ds(vi, 1), :]
                sn = s + sa * bt + vv * kt
                s_ref[vi] = sn
                parts.append(jnp.sum((sn * rt).reshape(nk // 8, 8, bh), axis=0))
            y_ref[te, 8 * g:8 * g + 8, :] = row_sums(parts)
        return pn

    p_end = lax.fori_loop(0, tc, step, jnp.ones((nk, bh), F32))
    for vi in range(nv):
        s_ref[vi] = s_ref[vi] * p_end


def _rw_scan_call(r, k, v, w, a, kk_tab, ka_tab, ncc, reverse):
    T, K, BH = r.shape
    tc = SCAN_TC
    NC = T // tc

    def cidx(c):
        return jnp.where(c < ncc, ncc - 1 - c, NC - 1 - (c - ncc)) if reverse else c

    sh_spec = pl.BlockSpec((tc, K, BH), lambda c: (cidx(c), 0, 0))
    return pl.pallas_call(
        functools.partial(_rw_scan_body, tc=tc, reverse=reverse),
        grid=(NC,),
        in_specs=[sh_spec] * 5 + [_const_spec((K, BH)), _const_spec((K, BH))],
        out_specs=sh_spec,
        out_shape=jax.ShapeDtypeStruct((T, K, BH), F32),
        scratch_shapes=[pltpu.VMEM((K, K, BH), F32)],
        compiler_params=_cparams(1),
        name="rw_scan_bwd" if reverse else "rw_scan_fwd",
    )(r, k, v, w, a, kk_tab, ka_tab)


def _rw_fin_body(y0_ref, y1_ref, r_ref, k_ref, v_ref, a0_ref, a1_ref, ka_ref, rk_ref, lg_ref, lb_ref, z_ref):
    y = y0_ref[...] + y1_ref[...]
    mean = jnp.mean(y, axis=1, keepdims=True)
    yc = y - mean
    var = jnp.mean(yc * yc, axis=1, keepdims=True)
    yn = yc * lax.rsqrt(var + RW_GN_EPS) * lg_ref[...] + lb_ref[...]
    k = k_ref[...]
    ka = ka_ref[...]
    ksum = k * (1.0 + (a0_ref[...] - 1.0) * ka) + k * (1.0 + (a1_ref[...] - 1.0) * ka)
    bonus = jnp.sum(r_ref[...] * ksum * rk_ref[...], axis=1, keepdims=True) * v_ref[...]
    z_ref[...] = yn + bonus


def _rw_fin_call(y0, y1, r, k, v, a0, a1, ka_tab, rk_tab, lg_tab, lb_tab):
    T, K, BH = r.shape
    tc = SCAN_TC
    sh_spec = pl.BlockSpec((tc, K, BH), lambda c: (c, 0, 0))
    tab = _const_spec((K, BH))
    return pl.pallas_call(
        _rw_fin_body,
        grid=(T // tc,),
        in_specs=[sh_spec] * 7 + [tab] * 4,
        out_specs=sh_spec,
        out_shape=jax.ShapeDtypeStruct((T, K, BH), F32),
        compiler_params=_cparams(1),
        name="rw_fin",
    )(y0, y1, r, k, v, a0, a1, ka_tab, rk_tab, lg_tab, lb_tab)


def _final_body(x_ref, g_ref, o_ref):
    x = x_ref[0]
    ms = jnp.mean(x * x, axis=-1, keepdims=True)
    o_ref[0] = x * lax.rsqrt(ms + NORM_EPS) * g_ref[...]


def _final_call(xa, g, nct):
    B, T, D = xa.shape
    N = T - nct * TM
    return pl.pallas_call(
        _final_body,
        grid=(B, N // TM),
        in_specs=[pl.BlockSpec((1, TM, D), lambda b, t: (b, t + nct, 0)), _const_spec((1, D))],
        out_specs=pl.BlockSpec((1, TM, D), lambda b, t: (b, t, 0)),
        out_shape=jax.ShapeDtypeStruct((B, N, D), F32),
        compiler_params=_cparams(2),
        name="final_norm",
    )(xa, g)


def _rope_tables(n_ctx, n_lat, rot_dim):
    mfreq = rot_dim // 4
    t = jnp.arange(n_lat)
    row = (t // GRID_W).astype(F32)
    col = (t % GRID_W).astype(F32)
    inv = ROPE_BASE ** (-jnp.arange(mfreq, dtype=F32) / mfreq)
    ar = row[:, None] * inv
    ac = col[:, None] * inv
    cos = jnp.concatenate([jnp.cos(ar), jnp.cos(ar), jnp.cos(ac), jnp.cos(ac)], axis=1)
    sin = jnp.concatenate([-jnp.sin(ar), jnp.sin(ar), -jnp.sin(ac), jnp.sin(ac)], axis=1)
    cos = jnp.concatenate([jnp.ones((n_ctx, rot_dim), F32), cos], axis=0)
    sin = jnp.concatenate([jnp.zeros((n_ctx, rot_dim), F32), sin], axis=0)
    return cos, sin


def _swap_perm(n, half):
    idx = jnp.arange(n)
    return idx ^ half


def _to_scan(x, B):
    lead = x.shape[:-3]
    T, D = x.shape[-2:]
    H = D // RW_HEAD
    x = x.reshape(lead + (B, T, H, RW_HEAD))
    nl = len(lead)
    perm = tuple(range(nl)) + (nl + 1, nl + 3, nl, nl + 2)
    return jnp.transpose(x, perm).reshape(lead + (T, RW_HEAD, B * H))


def _head_tab(p, B):
    H = p.size // RW_HEAD
    t = p.reshape(H, RW_HEAD).T
    return jnp.tile(t[:, None, :], (1, B, 1)).reshape(RW_HEAD, B * H).astype(F32)


def kernel(x, c, ctx, c_ctx, ada_w, ada_b, norm1_g, norm2_g, ffn_w1, ffn_w3, ffn_w2, final_norm_g, da_wqkv, da_lambda, da_subln_g, da_wo, rw_mu, rw_wrkv, rw_w0, rw_w1, rw_w2, rw_a0, rw_a1, rw_a2, rw_g1, rw_g2, rw_k_k, rw_k_a, rw_r_k, rw_lnx_g, rw_lnx_b, rw_wo, mla_wdown, mla_q_norm_g, mla_wuq, mla_kv_norm_g, mla_wukv, mla_wo):
    B, N, D = x.shape
    n_ctx = ctx.shape[1]
    depth = ada_w.shape[0]
    assert n_ctx % TM == 0 and N % TM == 0 and n_ctx % SCAN_TC == 0
    nct = n_ctx // TM
    T = n_ctx + N
    assert (T // TM) % KV_TILES == 0 and (T // TM // KV_TILES) % 2 == 1

    xa = jnp.concatenate([ctx, x], axis=1)
    R = -(-(B + 1) // 8) * 8
    cond = jnp.zeros((R, D), F32).at[:B].set(c).at[B].set(c_ctx)
    mod = _ada_call(cond, ada_w, ada_b).reshape(depth, R, 6, D)

    da_cos, da_sin = _rope_tables(n_ctx, N, DA_HEAD_DIM)
    da_scale = (DA_HEAD_DIM ** -0.5) * LOG2E
    da_cq, da_sq = (da_cos * da_scale).T, (da_sin * da_scale).T
    da_ck, da_sk = jnp.tile(da_cos, (1, 2)), jnp.tile(da_sin, (1, 2))
    ml_cos, ml_sin = _rope_tables(n_ctx, N, MLA_ROPE)
    ml_scale = ((MLA_NOPE + MLA_ROPE) ** -0.5) * LOG2E
    ml_cq, ml_sq = jnp.tile((ml_cos * ml_scale).T, (2, 1)), jnp.tile((ml_sin * ml_scale).T, (2, 1))
    zpad = jnp.zeros((T, 64), F32)
    ml_ck = jnp.concatenate([ml_cos, ml_cos, zpad], axis=1)
    ml_sk = jnp.concatenate([ml_sin, ml_sin, zpad], axis=1)

    for i in range(depth):
        kind, j = i % N_MIXERS, i // N_MIXERS
        mod_i = mod[i]
        g1 = norm1_g[i].reshape(1, D)
        gate = None
        if kind == 0:
            wq, wk, wv = jnp.split(da_wqkv[j], 3, axis=1)
            qt, kk, vt = _da_pre_call(xa, mod_i, g1, wq.T.astype(BF16), wk.astype(BF16),
                                      wv.T.astype(BF16), da_cq, da_sq, da_ck, da_sk, nct)
            lam_init = 0.8 - 0.6 * math.exp(-0.3 * i)
            o = _attn_call(da_lambda[j], da_subln_g[j].reshape(1, -1), qt, kk, vt,
                           mode="diff", nct=nct, lam_init=lam_init)
            wo = da_wo[j]
        elif kind == 1:
            H = D // RW_HEAD
            w1c = jnp.concatenate([rw_w1[j, 0], rw_w1[j, 1]], axis=1).astype(BF16)
            a1c = jnp.concatenate([rw_a1[j, 0], rw_a1[j, 1]], axis=1).astype(BF16)
            zl = jnp.zeros_like(rw_w2[j, 0])
            w2p = jnp.stack([jnp.concatenate([rw_w2[j, 0], zl], axis=0),
                             jnp.concatenate([zl, rw_w2[j, 1]], axis=0)]).astype(BF16)
            a2p = jnp.stack([jnp.concatenate([rw_a2[j, 0], zl], axis=0),
                             jnp.concatenate([zl, rw_a2[j, 1]], axis=0)]).astype(BF16)
            gl = rw_g1.shape[2]
            glp = -(-gl // 128) * 128
            g1p = jnp.pad(rw_g1[j], ((0, 0), (0, glp - gl))).astype(BF16)
            g2p = jnp.pad(rw_g2[j], ((0, glp - gl), (0, 0))).astype(BF16)
            *feats, gate = _rw_pre_call(
                xa, mod_i, g1, rw_mu[j], rw_wrkv[j, 0].astype(BF16), rw_wrkv[j, 1].astype(BF16),
                rw_wrkv[j, 2].astype(BF16), w1c, w2p, rw_w0[j], a1c, a2p, rw_a0[j], g1p, g2p, nct)
            rs, ks, vs, wf, wb, af, ab = (_to_scan(u, B) for u in feats)
            ka_tab = _head_tab(rw_k_a[j], B)
            kk_tab = _head_tab(rw_k_k[j], B)
            y0 = _rw_scan_call(rs, ks, vs, wf, af, kk_tab, ka_tab, n_ctx // SCAN_TC, False)
            y1 = _rw_scan_call(rs, ks, vs, wb, ab, kk_tab, ka_tab, n_ctx // SCAN_TC, True)
            z = _rw_fin_call(y0, y1, rs, ks, vs, af, ab, ka_tab, _head_tab(rw_r_k[j], B),
                             _head_tab(rw_lnx_g[j], B), _head_tab(rw_lnx_b[j], B))
            o = jnp.transpose(z.reshape(T, RW_HEAD, B, H), (2, 0, 3, 1)).reshape(B, T, D)
            wo = rw_wo[j]
        else:
            wd = mla_wdown[j]
            nlat = MLA_Q_LORA + MLA_KV_LORA
            kr = wd[:, nlat:]
            krs = kr[:, _swap_perm(MLA_ROPE, 8)]
            zc = jnp.zeros((D, 64), F32)
            wext = jnp.concatenate([wd[:, :nlat], kr, kr, zc, krs, krs, zc], axis=1).astype(BF16)
            wuq = mla_wuq[j].reshape(MLA_Q_LORA, MLA_HEADS // 2, 2, MLA_NOPE + MLA_ROPE)
            qn = wuq[..., :MLA_NOPE].reshape(MLA_Q_LORA, MLA_HEADS // 2, 2 * MLA_NOPE)
            qr = wuq[..., MLA_NOPE:].reshape(MLA_Q_LORA, MLA_HEADS // 2, 2 * MLA_ROPE)
            qz = jnp.zeros((MLA_Q_LORA, MLA_HEADS // 2, 64), F32)
            wuqt = jnp.concatenate([qn, qr, qz], axis=2).reshape(MLA_Q_LORA, -1).T.astype(BF16)
            wukv = mla_wukv[j].reshape(MLA_KV_LORA, MLA_HEADS, MLA_NOPE + MLA_V)
            wkn = wukv[..., :MLA_NOPE].reshape(MLA_KV_LORA, -1).astype(BF16)
            wvt = wukv[..., MLA_NOPE:].reshape(MLA_KV_LORA, -1).T.astype(BF16)
            qt, kk, vt = _mla_pre_call(xa, mod_i, g1, wext, mla_q_norm_g[j].reshape(1, -1),
                                       mla_kv_norm_g[j].reshape(1, -1), wuqt, wkn, wvt,
                                       ml_cq, ml_sq, ml_ck, ml_sk, nct, ml_scale)
            o = _attn_call(jnp.zeros((4, 64), F32), jnp.ones((1, 128), F32), qt, kk, vt,
                           mode="mla", nct=nct, lam_init=0.0)
            wo = mla_wo[j]
        xa = _post_call(xa, o, gate, mod_i, wo.astype(BF16), norm2_g[i].reshape(1, D),
                        ffn_w1[i].astype(BF16), ffn_w3[i].astype(BF16), ffn_w2[i].astype(BF16), nct)
    return _final_call(xa, final_norm_g.reshape(1, D), nct)
```

```python
import functools
import math

import jax
import jax.numpy as jnp
from jax import lax
from jax.experimental import pallas as pl
from jax.experimental.pallas import tpu as pltpu

F32 = jnp.float32
BF16 = jnp.bfloat16

GRID_W = 64
ROPE_BASE = 10000.0
NORM_EPS = 1e-6
N_MIXERS = 3
DA_HEAD_DIM = 64
DA_SUBLN_EPS = 1e-5
RW_HEAD = 64
RW_GN_EPS = 64e-5
MLA_HEADS = 16
MLA_Q_LORA = 256
MLA_KV_LORA = 128
MLA_NOPE = 64
MLA_ROPE = 32
MLA_V = 64
LOG2E = math.log2(math.e)

TM = 256
KV_TILES = 3
PAIR_UNROLL = 5
SCAN_TC = 32
ADA_TN = 1536
VMEM_LIMIT = 56 * 1024 * 1024


def _cparams(n_grid):
    return pltpu.CompilerParams(dimension_semantics=("arbitrary",) * n_grid, vmem_limit_bytes=VMEM_LIMIT)


def _const_spec(shape):
    nd = len(shape)
    return pl.BlockSpec(shape, lambda *_: (0,) * nd)


def _norm_mod(x, g, shift, scale):
    ms = jnp.mean(x * x, axis=-1, keepdims=True)
    return (x * lax.rsqrt(ms + NORM_EPS) * g) * (1.0 + scale) + shift


def _dot(a, b):
    return jnp.dot(a, b, preferred_element_type=F32)


def _dot_nt(a, b):
    return lax.dot_general(a, b, (((1,), (1,)), ((), ())), preferred_element_type=F32)


def _ada_body(c_ref, w_ref, b_ref, o_ref):
    c = c_ref[...]
    s = (c * jax.nn.sigmoid(c)).astype(BF16)
    o_ref[0] = _dot(s, w_ref[0].astype(BF16)) + b_ref[0]


def _ada_call(cond, ada_w, ada_b):
    L, D, D6 = ada_w.shape
    R = cond.shape[0]
    return pl.pallas_call(
        _ada_body,
        grid=(L, D6 // ADA_TN),
        in_specs=[
            pl.BlockSpec((R, D), lambda l, j: (0, 0)),
            pl.BlockSpec((1, D, ADA_TN), lambda l, j: (l, 0, j)),
            pl.BlockSpec((1, 1, ADA_TN), lambda l, j: (l, 0, j)),
        ],
        out_specs=pl.BlockSpec((1, R, ADA_TN), lambda l, j: (l, 0, j)),
        out_shape=jax.ShapeDtypeStruct((L, R, D6), F32),
        compiler_params=_cparams(2),
        name="ada_mod",
    )(cond, ada_w, ada_b.reshape(L, 1, D6))


def _tok_specs(B, nct, D):
    x_spec = pl.BlockSpec((1, TM, D), lambda b, t: (b, t, 0))
    mod_spec = pl.BlockSpec((1, 6, D), lambda b, t: (jnp.where(t < nct, B, b), 0, 0))
    return x_spec, mod_spec


def _da_pre_body(x_ref, mod_ref, g_ref, wqt_ref, wk_ref, wvt_ref, cq_ref, sq_ref, ck_ref, sk_ref,
                 qt_ref, k_ref, vt_ref):
    m = mod_ref[0]
    hb = _norm_mod(x_ref[0], g_ref[...], m[0:1], m[1:2]).astype(BF16)
    D = hb.shape[1]
    qt = _dot_nt(wqt_ref[...], hb)
    cq = cq_ref[...]
    sq = sq_ref[...]
    hd = DA_HEAD_DIM
    for h in range(D // hd):
        blk = [qt[h * hd + 16 * i:h * hd + 16 * (i + 1)] for i in range(4)]
        for i in range(4):
            r0 = 16 * i
            out = blk[i] * cq[r0:r0 + 16] + blk[i ^ 1] * sq[r0:r0 + 16]
            qt_ref[0, 0, h * hd + r0:h * hd + r0 + 16, :] = out.astype(BF16)
    k = _dot(hb, wk_ref[...])
    ck = ck_ref[...]
    sk = sk_ref[...]
    lane = lax.broadcasted_iota(jnp.int32, ck.shape, 1)
    first_half = (lane & 16) == 0
    for j in range(D // 128):
        kj = k[:, 128 * j:128 * (j + 1)]
        ks = jnp.where(first_half, pltpu.roll(kj, 112, axis=1), pltpu.roll(kj, 16, axis=1))
        k_ref[0, :, 128 * j:128 * (j + 1)] = (kj * ck + ks * sk).astype(BF16)
    vt_ref[0, 0] = _dot_nt(wvt_ref[...], hb).astype(BF16)


def _da_pre_call(xa, mod_i, g1, wqt, wk, wvt, cq, sq, ck, sk, nct):
    B, T, D = xa.shape
    NT = T // TM
    x_spec, mod_spec = _tok_specs(B, nct, D)
    fm_spec = pl.BlockSpec((1, 1, D, TM), lambda b, t: (b, t, 0, 0))
    return pl.pallas_call(
        _da_pre_body,
        grid=(B, NT),
        in_specs=[
            x_spec, mod_spec, _const_spec((1, D)),
            _const_spec((D, D)), _const_spec((D, D)), _const_spec((D, D)),
            pl.BlockSpec((64, TM), lambda b, t: (0, t)), pl.BlockSpec((64, TM), lambda b, t: (0, t)),
            pl.BlockSpec((TM, 128), lambda b, t: (t, 0)), pl.BlockSpec((TM, 128), lambda b, t: (t, 0)),
        ],
        out_specs=[fm_spec, x_spec, fm_spec],
        out_shape=[
            jax.ShapeDtypeStruct((B, NT, D, TM), BF16),
            jax.ShapeDtypeStruct((B, T, D), BF16),
            jax.ShapeDtypeStruct((B, NT, D, TM), BF16),
        ],
        compiler_params=_cparams(2),
        name="da_pre",
    )(xa, mod_i, g1, wqt, wk, wvt, cq, sq, ck, sk)


def _attn_body(lam_ref, g_ref, q_ref, k_ref, v_ref, o_ref, m_ref, l_ref, acc_ref, s_ref, mx_ref, q2_ref,
               *, mode, nct, nt, lam_init, kvt, unroll):
    dk, tq = q_ref.shape[2], q_ref.shape[3]

    def prep_q(qb):
        q = q_ref[0, qb].astype(F32)
        rows = lax.broadcasted_iota(jnp.int32, (dk, tq), 0)
        if mode == "diff":
            sel_a = rows < 64
            qa = jnp.where(sel_a, q, 0.0)
            qb_ = jnp.where(sel_a, 0.0, q)
        else:
            blk = rows // 32
            in_a = (blk == 0) | (blk == 1) | (blk == 4)
            in_b = (blk == 2) | (blk == 3) | (blk == 5)
            qa = jnp.where(in_a, q, 0.0)
            qb_ = jnp.where(in_b, q, 0.0)
        q2_ref[...] = jnp.concatenate([qa, qb_], axis=1).astype(BF16)

    def init():
        m_ref[...] = jnp.full(m_ref.shape, -1e30, F32)
        l_ref[...] = jnp.zeros(l_ref.shape, F32)
        acc_ref[...] = jnp.zeros(acc_ref.shape, F32)

    def qk(c0, ntile, slot):
        tk = ntile * TM
        k0 = c0 * TM if isinstance(c0, int) else pl.multiple_of(c0 * TM, TM)
        s = _dot(k_ref[0, pl.ds(k0, tk), :], q2_ref[...])
        s_ref[slot, 0:tk, 0:2 * tq] = s
        mx_ref[slot] = jnp.max(s, axis=0, keepdims=True)

    def softmax_pv(c0, ntile, slot):
        s = s_ref[slot, 0:ntile * TM, 0:2 * tq]
        m_prev = m_ref[...]
        m_new = jnp.maximum(m_prev, mx_ref[slot])
        alpha = jnp.exp2(m_prev - m_new)
        p = jnp.exp2(s - m_new)
        l_ref[...] = alpha * l_ref[...] + jnp.sum(p, axis=0, keepdims=True)
        pb = p.astype(BF16)
        vc = jnp.concatenate([v_ref[0, c0 + j] for j in range(ntile)], axis=1)
        acc_ref[...] = alpha * acc_ref[...] + _dot(vc, pb)
        m_ref[...] = m_new

    def finish(qb):
        a = acc_ref[...] * (1.0 / l_ref[...])
        if mode == "diff":
            lv = lam_ref[...]
            lam = (jnp.exp(jnp.sum(lv[0:1] * lv[1:2], keepdims=True))
                   - jnp.exp(jnp.sum(lv[2:3] * lv[3:4], keepdims=True)) + lam_init)
            o = a[:, :tq] - lam * a[:, tq:]
            ms = jnp.mean(o * o, axis=0, keepdims=True)
            o = o * lax.rsqrt(ms + DA_SUBLN_EPS)
            ot = (o.T * g_ref[...]) * (1.0 - lam_init)
        else:
            ot = jnp.concatenate([a[:64, :tq], a[64:, tq:]], axis=0).T
        if isinstance(qb, int):
            o_ref[0, qb * TM:(qb + 1) * TM, :] = ot.astype(BF16)
        else:
            o_ref[0, pl.ds(pl.multiple_of(qb * TM, TM), TM), :] = ot.astype(BF16)

    for qb in range(nct):
        prep_q(qb)
        init()
        for c in range(nct):
            qk(c, 1, 0)
            softmax_pv(c, 1, 0)
        finish(qb)

    n = nt // kvt
    nq = nt - nct
    last = (n - 1) * kvt
    npairs = (n - 1) // 2
    if npairs % unroll:
        unroll = 1

    def tile_core(slot0):
        init()

        def pairs(j, carry):
            for u in range(unroll):
                c = 2 * (j * unroll + u) * kvt
                qk(c + kvt, kvt, 1 - slot0)
                softmax_pv(c, kvt, slot0)
                qk(c + 2 * kvt, kvt, slot0)
                softmax_pv(c + kvt, kvt, 1 - slot0)
            return carry

        lax.fori_loop(0, npairs // unroll + jnp.minimum(pl.program_id(0), 0), pairs, 0)

    def tile_end(qb, slot0, qb_next):
        prep_q(qb_next)
        qk(0, kvt, 1 - slot0)
        softmax_pv(last, kvt, slot0)
        finish(qb)

    prep_q(nct)
    qk(0, kvt, 0)

    def two_tiles(i, carry):
        qb = nct + 2 * i
        tile_core(0)
        tile_end(qb, 0, qb + 1)
        tile_core(1)
        tile_end(qb + 1, 1, jnp.minimum(qb + 2, nt - 1))
        return carry

    lax.fori_loop(0, nq // 2, two_tiles, 0)
    if nq % 2:
        tile_core(0)
        softmax_pv(last, kvt, 0)
        finish(nt - 1)


def _attn_call(lam, g, qt, k, vt, *, mode, nct, lam_init, kvt=KV_TILES, unroll=PAIR_UNROLL):
    B, NT, GD, _ = qt.shape
    T = NT * TM
    G = vt.shape[2] // 128
    dk = GD // G
    body = functools.partial(_attn_body, mode=mode, nct=nct, nt=NT, lam_init=lam_init, kvt=kvt, unroll=unroll)
    return pl.pallas_call(
        body,
        grid=(B, G),
        in_specs=[
            _const_spec(lam.shape), _const_spec(g.shape),
            pl.BlockSpec((1, NT, dk, TM), lambda b, h: (b, 0, h, 0)),
            pl.BlockSpec((1, T, dk), lambda b, h: (b, 0, h)),
            pl.BlockSpec((1, NT, 128, TM), lambda b, h: (b, 0, h, 0)),
        ],
        out_specs=pl.BlockSpec((1, T, 128), lambda b, h: (b, 0, h)),
        out_shape=jax.ShapeDtypeStruct((B, T, G * 128), BF16),
        scratch_shapes=[
            pltpu.VMEM((1, 2 * TM), F32), pltpu.VMEM((1, 2 * TM), F32), pltpu.VMEM((128, 2 * TM), F32),
            pltpu.VMEM((2, kvt * TM, 2 * TM), F32), pltpu.VMEM((2, 1, 2 * TM), F32),
            pltpu.VMEM((dk, 2 * TM), BF16),
        ],
        compiler_params=_cparams(2),
        name="attn_" + mode,
    )(lam, g, qt, k, vt)


def _post_body(*refs, has_gate):
    if has_gate:
        x_ref, o_ref, gt_ref, mod_ref, wo_ref, n2_ref, w1_ref, w3_ref, w2_ref, out_ref = refs
        o = (o_ref[0] * gt_ref[0]).astype(BF16)
    else:
        x_ref, o_ref, mod_ref, wo_ref, n2_ref, w1_ref, w3_ref, w2_ref, out_ref = refs
        o = o_ref[0]
    m = mod_ref[0]
    x1 = x_ref[0] + m[2:3] * _dot(o, wo_ref[...])
    h = _norm_mod(x1, n2_ref[...], m[3:4], m[4:5]).astype(BF16)
    a = _dot(h, w1_ref[...])
    b = _dot(h, w3_ref[...])
    u = (a * jax.nn.sigmoid(a) * b).astype(BF16)
    out_ref[0] = x1 + m[5:6] * _dot(u, w2_ref[...])


def _post_call(xa, o, gate, mod_i, wo, n2, w1, w3, w2, nct):
    B, T, D = xa.shape
    F = w1.shape[1]
    x_spec, mod_spec = _tok_specs(B, nct, D)
    has_gate = gate is not None
    ins = [xa, o] + ([gate] if has_gate else []) + [mod_i, wo, n2, w1, w3, w2]
    one = pl.Buffered(1)
    specs = [x_spec, x_spec] + ([x_spec] if has_gate else []) + [
        mod_spec,
        pl.BlockSpec((D, D), lambda b, t: (0, 0), pipeline_mode=one),
        _const_spec((1, D)),
        pl.BlockSpec((D, F), lambda b, t: (0, 0), pipeline_mode=one),
        pl.BlockSpec((D, F), lambda b, t: (0, 0), pipeline_mode=one),
        pl.BlockSpec((F, D), lambda b, t: (0, 0), pipeline_mode=one),
    ]
    return pl.pallas_call(
        functools.partial(_post_body, has_gate=has_gate),
        grid=(B, T // TM),
        in_specs=specs,
        out_specs=x_spec,
        out_shape=jax.ShapeDtypeStruct((B, T, D), F32),
        input_output_aliases={0: 0},
        compiler_params=_cparams(2),
        name="post_ffn",
    )(*ins)


def _mla_pre_body(x_ref, mod_ref, g_ref, wext_ref, gq_ref, gkv_ref, wuqt_ref, wkn_ref, wvt_ref,
                  cq_ref, sq_ref, ck_ref, sk_ref, qt_ref, k_ref, vt_ref, *, scale):
    m = mod_ref[0]
    hb = _norm_mod(x_ref[0], g_ref[...], m[0:1], m[1:2]).astype(BF16)
    down = _dot(hb, wext_ref[...])
    ql = down[:, :MLA_Q_LORA]
    cq_lat = (ql * lax.rsqrt(jnp.mean(ql * ql, axis=-1, keepdims=True) + NORM_EPS) * gq_ref[...]).astype(BF16)
    kl = down[:, MLA_Q_LORA:MLA_Q_LORA + MLA_KV_LORA]
    ckv = (kl * lax.rsqrt(jnp.mean(kl * kl, axis=-1, keepdims=True) + NORM_EPS) * gkv_ref[...]).astype(BF16)
    kr = (down[:, 384:512] * ck_ref[...] + down[:, 512:640] * sk_ref[...]).astype(BF16)
    kn = _dot(ckv, wkn_ref[...])
    npair = kn.shape[1] // 128
    for p in range(npair):
        k_ref[0, :, 256 * p:256 * p + 128] = kn[:, 128 * p:128 * (p + 1)].astype(BF16)
        k_ref[0, :, 256 * p + 128:256 * (p + 1)] = kr
    vt_ref[0, 0] = _dot_nt(wvt_ref[...], ckv).astype(BF16)
    qt = _dot_nt(wuqt_ref[...], cq_lat)
    cq = cq_ref[...]
    sq = sq_ref[...]
    for p in range(npair):
        base = 256 * p
        qt_ref[0, 0, base:base + 128, :] = (qt[base:base + 128] * scale).astype(BF16)
        blk = [qt[base + 128 + 8 * i:base + 136 + 8 * i] for i in range(8)]
        for i in range(8):
            out = blk[i] * cq[8 * i:8 * i + 8] + blk[i ^ 1] * sq[8 * i:8 * i + 8]
            qt_ref[0, 0, base + 128 + 8 * i:base + 136 + 8 * i, :] = out.astype(BF16)
        qt_ref[0, 0, base + 192:base + 256, :] = jnp.zeros((64, qt.shape[1]), BF16)


def _mla_pre_call(xa, mod_i, g1, wext, gq, gkv, wuqt, wkn, wvt, cq, sq, ck, sk, nct, scale):
    B, T, D = xa.shape
    NT = T // TM
    x_spec, mod_spec = _tok_specs(B, nct, D)
    GD = wuqt.shape[0]
    return pl.pallas_call(
        functools.partial(_mla_pre_body, scale=scale),
        grid=(B, NT),
        in_specs=[
            x_spec, mod_spec, _const_spec((1, D)),
            _const_spec(wext.shape), _const_spec(gq.shape), _const_spec(gkv.shape),
            _const_spec(wuqt.shape), _const_spec(wkn.shape), _const_spec(wvt.shape),
            pl.BlockSpec((64, TM), lambda b, t: (0, t)), pl.BlockSpec((64, TM), lambda b, t: (0, t)),
            pl.BlockSpec((TM, 128), lambda b, t: (t, 0)), pl.BlockSpec((TM, 128), lambda b, t: (t, 0)),
        ],
        out_specs=[
            pl.BlockSpec((1, 1, GD, TM), lambda b, t: (b, t, 0, 0)),
            pl.BlockSpec((1, TM, GD), lambda b, t: (b, t, 0)),
            pl.BlockSpec((1, 1, D, TM), lambda b, t: (b, t, 0, 0)),
        ],
        out_shape=[
            jax.ShapeDtypeStruct((B, NT, GD, TM), BF16),
            jax.ShapeDtypeStruct((B, T, GD), BF16),
            jax.ShapeDtypeStruct((B, NT, D, TM), BF16),
        ],
        compiler_params=_cparams(2),
        name="mla_pre",
    )(xa, mod_i, g1, wext, gq, gkv, wuqt, wkn, wvt, cq, sq, ck, sk)


def _rw_pre_body(x_ref, xp_ref, xn_ref, mod_ref, g_ref, mu_ref, wr_ref, wk_ref, wv_ref, w1_ref, w2_ref, w0_ref,
                 a1_ref, a2_ref, a0_ref, g1_ref, g2_ref, r_ref, k_ref, v_ref, wf_ref, wb_ref, af_ref, ab_ref, gt_ref,
                 *, nct, nt):
    t = pl.program_id(1)
    m = mod_ref[0]
    g = g_ref[...]
    h = _norm_mod(x_ref[0], g, m[0:1], m[1:2])
    hp = _norm_mod(xp_ref[0], g, m[0:1], m[1:2])[7:8]
    hn = _norm_mod(xn_ref[0], g, m[0:1], m[1:2])[0:1]
    has_prev = jnp.logical_and(t != 0, t != nct)
    has_next = jnp.logical_and(t != nct - 1, t != nt - 1)
    hp = jnp.where(has_prev, hp, 0.0)
    hn = jnp.where(has_next, hn, 0.0)
    tm = h.shape[0]
    rid = lax.broadcasted_iota(jnp.int32, h.shape, 0)
    h_m1 = jnp.where(rid == 0, hp, pltpu.roll(h, 1, axis=0))
    h_p1 = jnp.where(rid == tm - 1, hn, pltpu.roll(h, tm - 1, axis=0))
    cs = 0.5 * (h_m1 + h_p1) - h
    mu = mu_ref[...]

    def mix(s):
        return (h + cs * mu[s:s + 1]).astype(BF16)

    r_ref[0] = _dot(mix(0), wr_ref[...])
    k_ref[0] = _dot(mix(1), wk_ref[...])
    v_ref[0] = _dot(mix(2), wv_ref[...])
    lw = jnp.tanh(_dot(mix(3), w1_ref[...])).astype(BF16)
    la = _dot(mix(4), a1_ref[...]).astype(BF16)
    for z in range(2):
        wl = w0_ref[z:z + 1] + _dot(lw, w2_ref[z])
        (wf_ref, wb_ref)[z][0] = jnp.exp(-math.exp(-0.5) * jax.nn.sigmoid(wl))
        (af_ref, ab_ref)[z][0] = jax.nn.sigmoid(a0_ref[z:z + 1] + _dot(la, a2_ref[z]))
    gl = jax.nn.sigmoid(_dot(mix(5), g1_ref[...])).astype(BF16)
    gt_ref[0] = _dot(gl, g2_ref[...])


def _rw_pre_call(xa, mod_i, g1n, mu, wr, wk, wv, w1, w2, w0, a1, a2, a0, g1, g2, nct):
    B, T, D = xa.shape
    NT = T // TM
    x_spec, mod_spec = _tok_specs(B, nct, D)
    r8 = TM // 8
    prev_spec = pl.BlockSpec((1, 8, D), lambda b, t: (b, jnp.maximum(t * r8 - 1, 0), 0))
    next_spec = pl.BlockSpec((1, 8, D), lambda b, t: (b, jnp.minimum((t + 1) * r8, T // 8 - 1), 0))
    consts = [g1n, mu, wr, wk, wv, w1, w2, w0, a1, a2, a0, g1, g2]
    tok = jax.ShapeDtypeStruct((B, T, D), F32)
    return pl.pallas_call(
        functools.partial(_rw_pre_body, nct=nct, nt=NT),
        grid=(B, NT),
        in_specs=[x_spec, prev_spec, next_spec, mod_spec] + [_const_spec(c.shape) for c in consts],
        out_specs=[x_spec] * 8,
        out_shape=[tok] * 8,
        compiler_params=_cparams(2),
        name="rw_pre",
    )(xa, xa, xa, mod_i, *consts)


def _rw_scan_body(*refs, tc, reverse, finish):
    r_ref, k_ref, v_ref, w_ref, a_ref, kk_ref, ka_ref = refs[:7]
    y_ref, s_ref = refs[-2:]

    @pl.when(pl.program_id(0) == 0)
    def _():
        s_ref[...] = jnp.zeros(s_ref.shape, F32)

    nv, nk, bh = s_ref.shape
    rowid = lax.broadcasted_iota(jnp.int32, (8, bh), 0)
    low4, low2, low1 = (rowid & 4) == 0, (rowid & 2) == 0, (rowid & 1) == 0

    def fold(x, y, keep, dist):
        u = jnp.where(keep, x, y)
        w = jnp.where(keep, y, x)
        if dist == 4:
            return u + pltpu.roll(w, 4, axis=0)
        return u + jnp.where(keep, pltpu.roll(w, 8 - dist, axis=0), pltpu.roll(w, dist, axis=0))

    def row_sums(parts):
        z = [fold(parts[i], parts[i + 4], low4, 4) for i in range(4)]
        y = [fold(z[i], z[i + 2], low2, 2) for i in range(2)]
        return fold(y[0], y[1], low1, 1)

    def step(i, p):
        te = tc - 1 - i if reverse else i
        r = r_ref[te]
        k = k_ref[te]
        a = a_ref[te]
        kk = k * kk_ref[...]
        nrm = jnp.sqrt(jnp.sum(kk * kk, axis=0, keepdims=True))
        kk = kk / jnp.maximum(nrm, 1e-12)
        pn = p * w_ref[te]
        ip = 1.0 / pn
        at = -kk * p
        bt = kk * a * ip
        kt = k * (1.0 + (a - 1.0) * ka_ref[...]) * ip
        rt = r * pn
        for g in range(nv // 8):
            parts = []
            for j in range(8):
                vi = 8 * g + j
                s = s_ref[vi]
                sa = jnp.sum(s * at, axis=0, keepdims=True)
                vv = v_ref[te, pl.ds(vi, 1), :]
                sn = s + sa * bt + vv * kt
                s_ref[vi] = sn
                parts.append(jnp.sum((sn * rt).reshape(nk // 8, 8, bh), axis=0))
            y_ref[te, 8 * g:8 * g + 8, :] = row_sums(parts)
        return pn

    p_end = lax.fori_loop(0, tc, step, jnp.ones((nk, bh), F32))
    for vi in range(nv):
        s_ref[vi] = s_ref[vi] * p_end

    if finish:
        yo_ref, ao_ref, rk_ref, lg_ref, lb_ref = refs[7:12]
        y = yo_ref[...] + y_ref[...]
        mean = jnp.mean(y, axis=1, keepdims=True)
        yc = y - mean
        var = jnp.mean(yc * yc, axis=1, keepdims=True)
        yn = yc * lax.rsqrt(var + RW_GN_EPS) * lg_ref[...] + lb_ref[...]
        k = k_ref[...]
        ka = ka_ref[...]
        ksum = k * (1.0 + (ao_ref[...] - 1.0) * ka) + k * (1.0 + (a_ref[...] - 1.0) * ka)
        bonus = jnp.sum(r_ref[...] * ksum * rk_ref[...], axis=1, keepdims=True) * v_ref[...]
        y_ref[...] = yn + bonus


def _rw_scan_call(r, k, v, w, a, kk_tab, ka_tab, ncc, reverse, fin=()):
    T, K, BH = r.shape
    tc = SCAN_TC
    NC = T // tc

    def cidx(c):
        return jnp.where(c < ncc, ncc - 1 - c, NC - 1 - (c - ncc)) if reverse else c

    sh_spec = pl.BlockSpec((tc, K, BH), lambda c: (cidx(c), 0, 0))
    tab = _const_spec((K, BH))
    return pl.pallas_call(
        functools.partial(_rw_scan_body, tc=tc, reverse=reverse, finish=bool(fin)),
        grid=(NC,),
        in_specs=[sh_spec] * 5 + [tab, tab] + ([sh_spec, sh_spec, tab, tab, tab] if fin else []),
        out_specs=sh_spec,
        out_shape=jax.ShapeDtypeStruct((T, K, BH), F32),
        scratch_shapes=[pltpu.VMEM((K, K, BH), F32)],
        compiler_params=_cparams(1),
        name="rw_scan_bwd" if reverse else "rw_scan_fwd",
    )(r, k, v, w, a, kk_tab, ka_tab, *fin)


def _final_body(x_ref, g_ref, o_ref):
    x = x_ref[0]
    ms = jnp.mean(x * x, axis=-1, keepdims=True)
    o_ref[0] = x * lax.rsqrt(ms + NORM_EPS) * g_ref[...]


def _final_call(xa, g, nct):
    B, T, D = xa.shape
    N = T - nct * TM
    return pl.pallas_call(
        _final_body,
        grid=(B, N // TM),
        in_specs=[pl.BlockSpec((1, TM, D), lambda b, t: (b, t + nct, 0)), _const_spec((1, D))],
        out_specs=pl.BlockSpec((1, TM, D), lambda b, t: (b, t, 0)),
        out_shape=jax.ShapeDtypeStruct((B, N, D), F32),
        compiler_params=_cparams(2),
        name="final_norm",
    )(xa, g)


def _rope_tables(n_ctx, n_lat, rot_dim):
    mfreq = rot_dim // 4
    t = jnp.arange(n_lat)
    row = (t // GRID_W).astype(F32)
    col = (t % GRID_W).astype(F32)
    inv = ROPE_BASE ** (-jnp.arange(mfreq, dtype=F32) / mfreq)
    ar = row[:, None] * inv
    ac = col[:, None] * inv
    cos = jnp.concatenate([jnp.cos(ar), jnp.cos(ar), jnp.cos(ac), jnp.cos(ac)], axis=1)
    sin = jnp.concatenate([-jnp.sin(ar), jnp.sin(ar), -jnp.sin(ac), jnp.sin(ac)], axis=1)
    cos = jnp.concatenate([jnp.ones((n_ctx, rot_dim), F32), cos], axis=0)
    sin = jnp.concatenate([jnp.zeros((n_ctx, rot_dim), F32), sin], axis=0)
    return cos, sin


def _swap_perm(n, half):
    idx = jnp.arange(n)
    return idx ^ half


def _to_scan(x, B):
    lead = x.shape[:-3]
    T, D = x.shape[-2:]
    H = D // RW_HEAD
    x = x.reshape(lead + (B, T, H, RW_HEAD))
    nl = len(lead)
    perm = tuple(range(nl)) + (nl + 1, nl + 3, nl, nl + 2)
    return jnp.transpose(x, perm).reshape(lead + (T, RW_HEAD, B * H))


def _head_tab(p, B):
    H = p.size // RW_HEAD
    t = p.reshape(H, RW_HEAD).T
    return jnp.tile(t[:, None, :], (1, B, 1)).reshape(RW_HEAD, B * H).astype(F32)


def kernel(x, c, ctx, c_ctx, ada_w, ada_b, norm1_g, norm2_g, ffn_w1, ffn_w3, ffn_w2, final_norm_g, da_wqkv, da_lambda, da_subln_g, da_wo, rw_mu, rw_wrkv, rw_w0, rw_w1, rw_w2, rw_a0, rw_a1, rw_a2, rw_g1, rw_g2, rw_k_k, rw_k_a, rw_r_k, rw_lnx_g, rw_lnx_b, rw_wo, mla_wdown, mla_q_norm_g, mla_wuq, mla_kv_norm_g, mla_wukv, mla_wo):
    B, N, D = x.shape
    n_ctx = ctx.shape[1]
    depth = ada_w.shape[0]
    assert n_ctx % TM == 0 and N % TM == 0 and n_ctx % SCAN_TC == 0
    nct = n_ctx // TM
    T = n_ctx + N
    assert (T // TM) % KV_TILES == 0 and (T // TM // KV_TILES) % 2 == 1

    xa = jnp.concatenate([ctx, x], axis=1)
    R = -(-(B + 1) // 8) * 8
    cond = jnp.zeros((R, D), F32).at[:B].set(c).at[B].set(c_ctx)
    mod = _ada_call(cond, ada_w, ada_b).reshape(depth, R, 6, D)

    da_cos, da_sin = _rope_tables(n_ctx, N, DA_HEAD_DIM)
    da_scale = (DA_HEAD_DIM ** -0.5) * LOG2E
    da_cq, da_sq = (da_cos * da_scale).T, (da_sin * da_scale).T
    da_ck, da_sk = jnp.tile(da_cos, (1, 2)), jnp.tile(da_sin, (1, 2))
    ml_cos, ml_sin = _rope_tables(n_ctx, N, MLA_ROPE)
    ml_scale = ((MLA_NOPE + MLA_ROPE) ** -0.5) * LOG2E
    ml_cq, ml_sq = jnp.tile((ml_cos * ml_scale).T, (2, 1)), jnp.tile((ml_sin * ml_scale).T, (2, 1))
    zpad = jnp.zeros((T, 64), F32)
    ml_ck = jnp.concatenate([ml_cos, ml_cos, zpad], axis=1)
    ml_sk = jnp.concatenate([ml_sin, ml_sin, zpad], axis=1)

    for i in range(depth):
        kind, j = i % N_MIXERS, i // N_MIXERS
        mod_i = mod[i]
        g1 = norm1_g[i].reshape(1, D)
        gate = None
        if kind == 0:
            wq, wk, wv = jnp.split(da_wqkv[j], 3, axis=1)
            qt, kk, vt = _da_pre_call(xa, mod_i, g1, wq.T.astype(BF16), wk.astype(BF16),
                                      wv.T.astype(BF16), da_cq, da_sq, da_ck, da_sk, nct)
            lam_init = 0.8 - 0.6 * math.exp(-0.3 * i)
            o = _attn_call(da_lambda[j], da_subln_g[j].reshape(1, -1), qt, kk, vt,
                           mode="diff", nct=nct, lam_init=lam_init)
            wo = da_wo[j]
        elif kind == 1:
            H = D // RW_HEAD
            w1c = jnp.concatenate([rw_w1[j, 0], rw_w1[j, 1]], axis=1).astype(BF16)
            a1c = jnp.concatenate([rw_a1[j, 0], rw_a1[j, 1]], axis=1).astype(BF16)
            zl = jnp.zeros_like(rw_w2[j, 0])
            w2p = jnp.stack([jnp.concatenate([rw_w2[j, 0], zl], axis=0),
                             jnp.concatenate([zl, rw_w2[j, 1]], axis=0)]).astype(BF16)
            a2p = jnp.stack([jnp.concatenate([rw_a2[j, 0], zl], axis=0),
                             jnp.concatenate([zl, rw_a2[j, 1]], axis=0)]).astype(BF16)
            gl = rw_g1.shape[2]
            glp = -(-gl // 128) * 128
            g1p = jnp.pad(rw_g1[j], ((0, 0), (0, glp - gl))).astype(BF16)
            g2p = jnp.pad(rw_g2[j], ((0, glp - gl), (0, 0))).astype(BF16)
            *feats, gate = _rw_pre_call(
                xa, mod_i, g1, rw_mu[j], rw_wrkv[j, 0].astype(BF16), rw_wrkv[j, 1].astype(BF16),
                rw_wrkv[j, 2].astype(BF16), w1c, w2p, rw_w0[j], a1c, a2p, rw_a0[j], g1p, g2p, nct)
            rs, ks, vs, wf, wb, af, ab = (_to_scan(u, B) for u in feats)
            ka_tab = _head_tab(rw_k_a[j], B)
            kk_tab = _head_tab(rw_k_k[j], B)
            y0 = _rw_scan_call(rs, ks, vs, wf, af, kk_tab, ka_tab, n_ctx // SCAN_TC, False)
            z = _rw_scan_call(rs, ks, vs, wb, ab, kk_tab, ka_tab, n_ctx // SCAN_TC, True,
                              fin=(y0, af, _head_tab(rw_r_k[j], B), _head_tab(rw_lnx_g[j], B),
                                   _head_tab(rw_lnx_b[j], B)))
            o = jnp.transpose(z.reshape(T, RW_HEAD, B, H), (2, 0, 3, 1)).reshape(B, T, D)
            wo = rw_wo[j]
        else:
            wd = mla_wdown[j]
            nlat = MLA_Q_LORA + MLA_KV_LORA
            kr = wd[:, nlat:]
            krs = kr[:, _swap_perm(MLA_ROPE, 8)]
            zc = jnp.zeros((D, 64), F32)
            wext = jnp.concatenate([wd[:, :nlat], kr, kr, zc, krs, krs, zc], axis=1).astype(BF16)
            wuq = mla_wuq[j].reshape(MLA_Q_LORA, MLA_HEADS // 2, 2, MLA_NOPE + MLA_ROPE)
            qn = wuq[..., :MLA_NOPE].reshape(MLA_Q_LORA, MLA_HEADS // 2, 2 * MLA_NOPE)
            qr = wuq[..., MLA_NOPE:].reshape(MLA_Q_LORA, MLA_HEADS // 2, 2 * MLA_ROPE)
            qz = jnp.zeros((MLA_Q_LORA, MLA_HEADS // 2, 64), F32)
            wuqt = jnp.concatenate([qn, qr, qz], axis=2).reshape(MLA_Q_LORA, -1).T.astype(BF16)
            wukv = mla_wukv[j].reshape(MLA_KV_LORA, MLA_HEADS, MLA_NOPE + MLA_V)
            wkn = wukv[..., :MLA_NOPE].reshape(MLA_KV_LORA, -1).astype(BF16)
            wvt = wukv[..., MLA_NOPE:].reshape(MLA_KV_LORA, -1).T.astype(BF16)
            qt, kk, vt = _mla_pre_call(xa, mod_i, g1, wext, mla_q_norm_g[j].reshape(1, -1),
                                       mla_kv_norm_g[j].reshape(1, -1), wuqt, wkn, wvt,
                                       ml_cq, ml_sq, ml_ck, ml_sk, nct, ml_scale)
            o = _attn_call(jnp.zeros((4, 64), F32), jnp.ones((1, 128), F32), qt, kk, vt,
                           mode="mla", nct=nct, lam_init=0.0)
            wo = mla_wo[j]
        xa = _post_call(xa, o, gate, mod_i, wo.astype(BF16), norm2_g[i].reshape(1, D),
                        ffn_w1[i].astype(BF16), ffn_w3[i].astype(BF16), ffn_w2[i].astype(BF16), nct)
    return _final_call(xa, final_norm_g.reshape(1, D), nct)
```

```python
import functools
import math

import jax
import jax.numpy as jnp
from jax import lax
from jax.experimental import pallas as pl
from jax.experimental.pallas import tpu as pltpu

F32 = jnp.float32
BF16 = jnp.bfloat16

GRID_W = 64
ROPE_BASE = 10000.0
NORM_EPS = 1e-6
N_MIXERS = 3
DA_HEAD_DIM = 64
DA_SUBLN_EPS = 1e-5
RW_HEAD = 64
RW_GN_EPS = 64e-5
MLA_HEADS = 16
MLA_Q_LORA = 256
MLA_KV_LORA = 128
MLA_NOPE = 64
MLA_ROPE = 32
MLA_V = 64
LOG2E = math.log2(math.e)

TM = 256
KV_TILES = 3
PAIR_UNROLL = 5
SCAN_TC = 32
ADA_TN = 1536
VMEM_LIMIT = 56 * 1024 * 1024


def _cparams(n_grid):
    return pltpu.CompilerParams(dimension_semantics=("arbitrary",) * n_grid, vmem_limit_bytes=VMEM_LIMIT)


def _const_spec(shape):
    nd = len(shape)
    return pl.BlockSpec(shape, lambda *_: (0,) * nd)


def _norm_mod(x, g, shift, scale):
    ms = jnp.mean(x * x, axis=-1, keepdims=True)
    return (x * lax.rsqrt(ms + NORM_EPS) * g) * (1.0 + scale) + shift


def _dot(a, b):
    return jnp.dot(a, b, preferred_element_type=F32)


def _dot_nt(a, b):
    return lax.dot_general(a, b, (((1,), (1,)), ((), ())), preferred_element_type=F32)


def _ada_body(c_ref, w_ref, b_ref, o_ref):
    c = c_ref[...]
    s = (c * jax.nn.sigmoid(c)).astype(BF16)
    o_ref[0] = _dot(s, w_ref[0].astype(BF16)) + b_ref[0]


def _ada_call(cond, ada_w, ada_b):
    L, D, D6 = ada_w.shape
    R = cond.shape[0]
    return pl.pallas_call(
        _ada_body,
        grid=(L, D6 // ADA_TN),
        in_specs=[
            pl.BlockSpec((R, D), lambda l, j: (0, 0)),
            pl.BlockSpec((1, D, ADA_TN), lambda l, j: (l, 0, j)),
            pl.BlockSpec((1, 1, ADA_TN), lambda l, j: (l, 0, j)),
        ],
        out_specs=pl.BlockSpec((1, R, ADA_TN), lambda l, j: (l, 0, j)),
        out_shape=jax.ShapeDtypeStruct((L, R, D6), F32),
        compiler_params=_cparams(2),
        name="ada_mod",
    )(cond, ada_w, ada_b.reshape(L, 1, D6))


def _tok_specs(B, nct, D):
    x_spec = pl.BlockSpec((1, TM, D), lambda b, t: (b, t, 0))
    mod_spec = pl.BlockSpec((1, 6, D), lambda b, t: (jnp.where(t < nct, B, b), 0, 0))
    return x_spec, mod_spec


def _da_pre_body(x_ref, mod_ref, g_ref, wqt_ref, wk_ref, wvt_ref, cq_ref, sq_ref, ck_ref, sk_ref,
                 qt_ref, k_ref, vt_ref):
    m = mod_ref[0]
    hb = _norm_mod(x_ref[0], g_ref[...], m[0:1], m[1:2]).astype(BF16)
    D = hb.shape[1]
    qt = _dot_nt(wqt_ref[...], hb)
    cq = cq_ref[...]
    sq = sq_ref[...]
    hd = DA_HEAD_DIM
    for h in range(D // hd):
        blk = [qt[h * hd + 16 * i:h * hd + 16 * (i + 1)] for i in range(4)]
        for i in range(4):
            r0 = 16 * i
            out = blk[i] * cq[r0:r0 + 16] + blk[i ^ 1] * sq[r0:r0 + 16]
            qt_ref[0, 0, h * hd + r0:h * hd + r0 + 16, :] = out.astype(BF16)
    k = _dot(hb, wk_ref[...])
    ck = ck_ref[...]
    sk = sk_ref[...]
    lane = lax.broadcasted_iota(jnp.int32, ck.shape, 1)
    first_half = (lane & 16) == 0
    for j in range(D // 128):
        kj = k[:, 128 * j:128 * (j + 1)]
        ks = jnp.where(first_half, pltpu.roll(kj, 112, axis=1), pltpu.roll(kj, 16, axis=1))
        k_ref[0, :, 128 * j:128 * (j + 1)] = (kj * ck + ks * sk).astype(BF16)
    vt_ref[0, 0] = _dot_nt(wvt_ref[...], hb).astype(BF16)


def _da_pre_call(xa, mod_i, g1, wqt, wk, wvt, cq, sq, ck, sk, nct):
    B, T, D = xa.shape
    NT = T // TM
    x_spec, mod_spec = _tok_specs(B, nct, D)
    fm_spec = pl.BlockSpec((1, 1, D, TM), lambda b, t: (b, t, 0, 0))
    return pl.pallas_call(
        _da_pre_body,
        grid=(B, NT),
        in_specs=[
            x_spec, mod_spec, _const_spec((1, D)),
            _const_spec((D, D)), _const_spec((D, D)), _const_spec((D, D)),
            pl.BlockSpec((64, TM), lambda b, t: (0, t)), pl.BlockSpec((64, TM), lambda b, t: (0, t)),
            pl.BlockSpec((TM, 128), lambda b, t: (t, 0)), pl.BlockSpec((TM, 128), lambda b, t: (t, 0)),
        ],
        out_specs=[fm_spec, x_spec, fm_spec],
        out_shape=[
            jax.ShapeDtypeStruct((B, NT, D, TM), BF16),
            jax.ShapeDtypeStruct((B, T, D), BF16),
            jax.ShapeDtypeStruct((B, NT, D, TM), BF16),
        ],
        compiler_params=_cparams(2),
        name="da_pre",
    )(xa, mod_i, g1, wqt, wk, wvt, cq, sq, ck, sk)


def _attn_body(lam_ref, g_ref, q_ref, k_ref, v_ref, o_ref, m_ref, l_ref, acc_ref, s_ref, mx_ref, q2_ref,
               *, mode, nct, nt, lam_init, kvt, unroll):
    dk, tq = q_ref.shape[2], q_ref.shape[3]

    def prep_q(qb):
        q = q_ref[0, qb].astype(F32)
        rows = lax.broadcasted_iota(jnp.int32, (dk, tq), 0)
        if mode == "diff":
            sel_a = rows < 64
            qa = jnp.where(sel_a, q, 0.0)
            qb_ = jnp.where(sel_a, 0.0, q)
        else:
            blk = rows // 32
            in_a = (blk == 0) | (blk == 1) | (blk == 4)
            in_b = (blk == 2) | (blk == 3) | (blk == 5)
            qa = jnp.where(in_a, q, 0.0)
            qb_ = jnp.where(in_b, q, 0.0)
        q2_ref[...] = jnp.concatenate([qa, qb_], axis=1).astype(BF16)

    def init():
        m_ref[...] = jnp.full(m_ref.shape, -1e30, F32)
        l_ref[...] = jnp.zeros(l_ref.shape, F32)
        acc_ref[...] = jnp.zeros(acc_ref.shape, F32)

    def qk(c0, ntile, slot):
        tk = ntile * TM
        k0 = c0 * TM if isinstance(c0, int) else pl.multiple_of(c0 * TM, TM)
        s = _dot(k_ref[0, pl.ds(k0, tk), :], q2_ref[...])
        s_ref[slot, 0:tk, 0:2 * tq] = s
        mx_ref[slot] = jnp.max(s, axis=0, keepdims=True)

    def softmax_pv(c0, ntile, slot):
        s = s_ref[slot, 0:ntile * TM, 0:2 * tq]
        m_prev = m_ref[...]
        m_new = jnp.maximum(m_prev, mx_ref[slot])
        alpha = jnp.exp2(m_prev - m_new)
        p = jnp.exp2(s - m_new)
        l_ref[...] = alpha * l_ref[...] + jnp.sum(p, axis=0, keepdims=True)
        pb = p.astype(BF16)
        vc = jnp.concatenate([v_ref[0, c0 + j] for j in range(ntile)], axis=1)
        acc_ref[...] = alpha * acc_ref[...] + _dot(vc, pb)
        m_ref[...] = m_new

    def finish(qb):
        a = acc_ref[...] * (1.0 / l_ref[...])
        if mode == "diff":
            lv = lam_ref[...]
            lam = (jnp.exp(jnp.sum(lv[0:1] * lv[1:2], keepdims=True))
                   - jnp.exp(jnp.sum(lv[2:3] * lv[3:4], keepdims=True)) + lam_init)
            o = a[:, :tq] - lam * a[:, tq:]
            ms = jnp.mean(o * o, axis=0, keepdims=True)
            o = o * lax.rsqrt(ms + DA_SUBLN_EPS)
            ot = (o.T * g_ref[...]) * (1.0 - lam_init)
        else:
            ot = jnp.concatenate([a[:64, :tq], a[64:, tq:]], axis=0).T
        if isinstance(qb, int):
            o_ref[0, qb * TM:(qb + 1) * TM, :] = ot.astype(BF16)
        else:
            o_ref[0, pl.ds(pl.multiple_of(qb * TM, TM), TM), :] = ot.astype(BF16)

    for qb in range(nct):
        prep_q(qb)
        init()
        for c in range(nct):
            qk(c, 1, 0)
            softmax_pv(c, 1, 0)
        finish(qb)

    n = nt // kvt
    nq = nt - nct
    last = (n - 1) * kvt
    npairs = (n - 1) // 2
    if npairs % unroll:
        unroll = 1

    def tile_core(slot0):
        init()

        def pairs(j, carry):
            for u in range(unroll):
                c = 2 * (j * unroll + u) * kvt
                qk(c + kvt, kvt, 1 - slot0)
                softmax_pv(c, kvt, slot0)
                qk(c + 2 * kvt, kvt, slot0)
                softmax_pv(c + kvt, kvt, 1 - slot0)
            return carry

        lax.fori_loop(0, npairs // unroll + jnp.minimum(pl.program_id(0), 0), pairs, 0)

    def tile_end(qb, slot0, qb_next):
        prep_q(qb_next)
        qk(0, kvt, 1 - slot0)
        softmax_pv(last, kvt, slot0)
        finish(qb)

    prep_q(nct)
    qk(0, kvt, 0)

    def two_tiles(i, carry):
        qb = nct + 2 * i
        tile_core(0)
        tile_end(qb, 0, qb + 1)
        tile_core(1)
        tile_end(qb + 1, 1, jnp.minimum(qb + 2, nt - 1))
        return carry

    lax.fori_loop(0, nq // 2, two_tiles, 0)
    if nq % 2:
        tile_core(0)
        softmax_pv(last, kvt, 0)
        finish(nt - 1)


def _attn_call(lam, g, qt, k, vt, *, mode, nct, lam_init, kvt=KV_TILES, unroll=PAIR_UNROLL):
    B, NT, GD, _ = qt.shape
    T = NT * TM
    G = vt.shape[2] // 128
    dk = GD // G
    body = functools.partial(_attn_body, mode=mode, nct=nct, nt=NT, lam_init=lam_init, kvt=kvt, unroll=unroll)
    return pl.pallas_call(
        body,
        grid=(B, G),
        in_specs=[
            _const_spec(lam.shape), _const_spec(g.shape),
            pl.BlockSpec((1, NT, dk, TM), lambda b, h: (b, 0, h, 0)),
            pl.BlockSpec((1, T, dk), lambda b, h: (b, 0, h)),
            pl.BlockSpec((1, NT, 128, TM), lambda b, h: (b, 0, h, 0)),
        ],
        out_specs=pl.BlockSpec((1, T, 128), lambda b, h: (b, 0, h)),
        out_shape=jax.ShapeDtypeStruct((B, T, G * 128), BF16),
        scratch_shapes=[
            pltpu.VMEM((1, 2 * TM), F32), pltpu.VMEM((1, 2 * TM), F32), pltpu.VMEM((128, 2 * TM), F32),
            pltpu.VMEM((2, kvt * TM, 2 * TM), F32), pltpu.VMEM((2, 1, 2 * TM), F32),
            pltpu.VMEM((dk, 2 * TM), BF16),
        ],
        compiler_params=_cparams(2),
        name="attn_" + mode,
    )(lam, g, qt, k, vt)


def _post_body(*refs, has_gate, has_final):
    x_ref, o_ref = refs[:2]
    nxt = 3 if has_gate else 2
    o = (o_ref[0] * refs[2][0]).astype(BF16) if has_gate else o_ref[0]
    mod_ref, wo_ref, n2_ref, w1_ref, w3_ref, w2_ref = refs[nxt:nxt + 6]
    out_ref = refs[-1]
    m = mod_ref[0]
    x1 = x_ref[0] + m[2:3] * _dot(o, wo_ref[...])
    h = _norm_mod(x1, n2_ref[...], m[3:4], m[4:5]).astype(BF16)
    a = _dot(h, w1_ref[...])
    b = _dot(h, w3_ref[...])
    u = (a * jax.nn.sigmoid(a) * b).astype(BF16)
    x2 = x1 + m[5:6] * _dot(u, w2_ref[...])
    if has_final:
        ms = jnp.mean(x2 * x2, axis=-1, keepdims=True)
        x2 = x2 * lax.rsqrt(ms + NORM_EPS) * refs[nxt + 6][...]
    out_ref[0] = x2


def _post_call(xa, o, gate, mod_i, wo, n2, w1, w3, w2, nct, final_g=None):
    B, T, D = xa.shape
    F = w1.shape[1]
    x_spec, mod_spec = _tok_specs(B, nct, D)
    has_gate = gate is not None
    has_final = final_g is not None
    ins = [xa, o] + ([gate] if has_gate else []) + [mod_i, wo, n2, w1, w3, w2] + ([final_g] if has_final else [])
    one = pl.Buffered(1)
    specs = [x_spec, x_spec] + ([x_spec] if has_gate else []) + [
        mod_spec,
        pl.BlockSpec((D, D), lambda b, t: (0, 0), pipeline_mode=one),
        _const_spec((1, D)),
        pl.BlockSpec((D, F), lambda b, t: (0, 0), pipeline_mode=one),
        pl.BlockSpec((D, F), lambda b, t: (0, 0), pipeline_mode=one),
        pl.BlockSpec((F, D), lambda b, t: (0, 0), pipeline_mode=one),
    ] + ([_const_spec((1, D))] if has_final else [])
    if has_final:
        out_spec = pl.BlockSpec((1, TM, D), lambda b, t: (b, jnp.maximum(t - nct, 0), 0))
        out_shape = jax.ShapeDtypeStruct((B, T - nct * TM, D), F32)
    else:
        out_spec, out_shape = x_spec, jax.ShapeDtypeStruct((B, T, D), F32)
    return pl.pallas_call(
        functools.partial(_post_body, has_gate=has_gate, has_final=has_final),
        grid=(B, T // TM),
        in_specs=specs,
        out_specs=out_spec,
        out_shape=out_shape,
        input_output_aliases={} if has_final else {0: 0},
        compiler_params=_cparams(2),
        name="post_ffn_final" if has_final else "post_ffn",
    )(*ins)


def _mla_pre_body(x_ref, mod_ref, g_ref, wext_ref, gq_ref, gkv_ref, wuqt_ref, wkn_ref, wvt_ref,
                  cq_ref, sq_ref, ck_ref, sk_ref, qt_ref, k_ref, vt_ref, *, scale):
    m = mod_ref[0]
    hb = _norm_mod(x_ref[0], g_ref[...], m[0:1], m[1:2]).astype(BF16)
    down = _dot(hb, wext_ref[...])
    ql = down[:, :MLA_Q_LORA]
    cq_lat = (ql * lax.rsqrt(jnp.mean(ql * ql, axis=-1, keepdims=True) + NORM_EPS) * gq_ref[...]).astype(BF16)
    kl = down[:, MLA_Q_LORA:MLA_Q_LORA + MLA_KV_LORA]
    ckv = (kl * lax.rsqrt(jnp.mean(kl * kl, axis=-1, keepdims=True) + NORM_EPS) * gkv_ref[...]).astype(BF16)
    kr = (down[:, 384:512] * ck_ref[...] + down[:, 512:640] * sk_ref[...]).astype(BF16)
    kn = _dot(ckv, wkn_ref[...])
    npair = kn.shape[1] // 128
    for p in range(npair):
        k_ref[0, :, 256 * p:256 * p + 128] = kn[:, 128 * p:128 * (p + 1)].astype(BF16)
        k_ref[0, :, 256 * p + 128:256 * (p + 1)] = kr
    vt_ref[0, 0] = _dot_nt(wvt_ref[...], ckv).astype(BF16)
    qt = _dot_nt(wuqt_ref[...], cq_lat)
    cq = cq_ref[...]
    sq = sq_ref[...]
    for p in range(npair):
        base = 256 * p
        qt_ref[0, 0, base:base + 128, :] = (qt[base:base + 128] * scale).astype(BF16)
        blk = [qt[base + 128 + 8 * i:base + 136 + 8 * i] for i in range(8)]
        for i in range(8):
            out = blk[i] * cq[8 * i:8 * i + 8] + blk[i ^ 1] * sq[8 * i:8 * i + 8]
            qt_ref[0, 0, base + 128 + 8 * i:base + 136 + 8 * i, :] = out.astype(BF16)
        qt_ref[0, 0, base + 192:base + 256, :] = jnp.zeros((64, qt.shape[1]), BF16)


def _mla_pre_call(xa, mod_i, g1, wext, gq, gkv, wuqt, wkn, wvt, cq, sq, ck, sk, nct, scale):
    B, T, D = xa.shape
    NT = T // TM
    x_spec, mod_spec = _tok_specs(B, nct, D)
    GD = wuqt.shape[0]
    return pl.pallas_call(
        functools.partial(_mla_pre_body, scale=scale),
        grid=(B, NT),
        in_specs=[
            x_spec, mod_spec, _const_spec((1, D)),
            _const_spec(wext.shape), _const_spec(gq.shape), _const_spec(gkv.shape),
            _const_spec(wuqt.shape), _const_spec(wkn.shape), _const_spec(wvt.shape),
            pl.BlockSpec((64, TM), lambda b, t: (0, t)), pl.BlockSpec((64, TM), lambda b, t: (0, t)),
            pl.BlockSpec((TM, 128), lambda b, t: (t, 0)), pl.BlockSpec((TM, 128), lambda b, t: (t, 0)),
        ],
        out_specs=[
            pl.BlockSpec((1, 1, GD, TM), lambda b, t: (b, t, 0, 0)),
            pl.BlockSpec((1, TM, GD), lambda b, t: (b, t, 0)),
            pl.BlockSpec((1, 1, D, TM), lambda b, t: (b, t, 0, 0)),
        ],
        out_shape=[
            jax.ShapeDtypeStruct((B, NT, GD, TM), BF16),
            jax.ShapeDtypeStruct((B, T, GD), BF16),
            jax.ShapeDtypeStruct((B, NT, D, TM), BF16),
        ],
        compiler_params=_cparams(2),
        name="mla_pre",
    )(xa, mod_i, g1, wext, gq, gkv, wuqt, wkn, wvt, cq, sq, ck, sk)


def _rw_pre_body(x_ref, xp_ref, xn_ref, mod_ref, g_ref, mu_ref, wr_ref, wk_ref, wv_ref, w1_ref, w2_ref, w0_ref,
                 a1_ref, a2_ref, a0_ref, g1_ref, g2_ref, r_ref, k_ref, v_ref, wf_ref, wb_ref, af_ref, ab_ref, gt_ref,
                 *, nct, nt):
    t = pl.program_id(1)
    m = mod_ref[0]
    g = g_ref[...]
    h = _norm_mod(x_ref[0], g, m[0:1], m[1:2])
    hp = _norm_mod(xp_ref[0], g, m[0:1], m[1:2])[7:8]
    hn = _norm_mod(xn_ref[0], g, m[0:1], m[1:2])[0:1]
    has_prev = jnp.logical_and(t != 0, t != nct)
    has_next = jnp.logical_and(t != nct - 1, t != nt - 1)
    hp = jnp.where(has_prev, hp, 0.0)
    hn = jnp.where(has_next, hn, 0.0)
    tm = h.shape[0]
    rid = lax.broadcasted_iota(jnp.int32, h.shape, 0)
    h_m1 = jnp.where(rid == 0, hp, pltpu.roll(h, 1, axis=0))
    h_p1 = jnp.where(rid == tm - 1, hn, pltpu.roll(h, tm - 1, axis=0))
    cs = 0.5 * (h_m1 + h_p1) - h
    mu = mu_ref[...]

    def mix(s):
        return (h + cs * mu[s:s + 1]).astype(BF16)

    r_ref[0] = _dot(mix(0), wr_ref[...])
    k_ref[0] = _dot(mix(1), wk_ref[...])
    v_ref[0] = _dot(mix(2), wv_ref[...])
    lw = jnp.tanh(_dot(mix(3), w1_ref[...])).astype(BF16)
    la = _dot(mix(4), a1_ref[...]).astype(BF16)
    for z in range(2):
        wl = w0_ref[z:z + 1] + _dot(lw, w2_ref[z])
        (wf_ref, wb_ref)[z][0] = jnp.exp(-math.exp(-0.5) * jax.nn.sigmoid(wl))
        (af_ref, ab_ref)[z][0] = jax.nn.sigmoid(a0_ref[z:z + 1] + _dot(la, a2_ref[z]))
    gl = jax.nn.sigmoid(_dot(mix(5), g1_ref[...])).astype(BF16)
    gt_ref[0] = _dot(gl, g2_ref[...])


def _rw_pre_call(xa, mod_i, g1n, mu, wr, wk, wv, w1, w2, w0, a1, a2, a0, g1, g2, nct):
    B, T, D = xa.shape
    NT = T // TM
    x_spec, mod_spec = _tok_specs(B, nct, D)
    r8 = TM // 8
    prev_spec = pl.BlockSpec((1, 8, D), lambda b, t: (b, jnp.maximum(t * r8 - 1, 0), 0))
    next_spec = pl.BlockSpec((1, 8, D), lambda b, t: (b, jnp.minimum((t + 1) * r8, T // 8 - 1), 0))
    consts = [g1n, mu, wr, wk, wv, w1, w2, w0, a1, a2, a0, g1, g2]
    tok = jax.ShapeDtypeStruct((B, T, D), F32)
    return pl.pallas_call(
        functools.partial(_rw_pre_body, nct=nct, nt=NT),
        grid=(B, NT),
        in_specs=[x_spec, prev_spec, next_spec, mod_spec] + [_const_spec(c.shape) for c in consts],
        out_specs=[x_spec] * 8,
        out_shape=[tok] * 8,
        compiler_params=_cparams(2),
        name="rw_pre",
    )(xa, xa, xa, mod_i, *consts)


def _rw_scan_body(*refs, tc, reverse, finish):
    r_ref, k_ref, v_ref, w_ref, a_ref, kk_ref, ka_ref = refs[:7]
    y_ref, s_ref = refs[-2:]

    @pl.when(pl.program_id(0) == 0)
    def _():
        s_ref[...] = jnp.zeros(s_ref.shape, F32)

    nv, nk, bh = s_ref.shape
    rowid = lax.broadcasted_iota(jnp.int32, (8, bh), 0)
    low4, low2, low1 = (rowid & 4) == 0, (rowid & 2) == 0, (rowid & 1) == 0

    def fold(x, y, keep, dist):
        u = jnp.where(keep, x, y)
        w = jnp.where(keep, y, x)
        if dist == 4:
            return u + pltpu.roll(w, 4, axis=0)
        return u + jnp.where(keep, pltpu.roll(w, 8 - dist, axis=0), pltpu.roll(w, dist, axis=0))

    def row_sums(parts):
        z = [fold(parts[i], parts[i + 4], low4, 4) for i in range(4)]
        y = [fold(z[i], z[i + 2], low2, 2) for i in range(2)]
        return fold(y[0], y[1], low1, 1)

    def step(i, p):
        te = tc - 1 - i if reverse else i
        r = r_ref[te]
        k = k_ref[te]
        a = a_ref[te]
        kk = k * kk_ref[...]
        nrm = jnp.sqrt(jnp.sum(kk * kk, axis=0, keepdims=True))
        kk = kk / jnp.maximum(nrm, 1e-12)
        pn = p * w_ref[te]
        ip = 1.0 / pn
        at = -kk * p
        bt = kk * a * ip
        kt = k * (1.0 + (a - 1.0) * ka_ref[...]) * ip
        rt = r * pn
        for g in range(nv // 8):
            parts = []
            for j in range(8):
                vi = 8 * g + j
                s = s_ref[vi]
                sa = jnp.sum(s * at, axis=0, keepdims=True)
                vv = v_ref[te, pl.ds(vi, 1), :]
                sn = s + sa * bt + vv * kt
                s_ref[vi] = sn
                parts.append(jnp.sum((sn * rt).reshape(nk // 8, 8, bh), axis=0))
            y_ref[te, 8 * g:8 * g + 8, :] = row_sums(parts)
        return pn

    p_end = lax.fori_loop(0, tc, step, jnp.ones((nk, bh), F32))
    for vi in range(nv):
        s_ref[vi] = s_ref[vi] * p_end

    if finish:
        yo_ref, ao_ref, rk_ref, lg_ref, lb_ref = refs[7:12]
        y = yo_ref[...] + y_ref[...]
        mean = jnp.mean(y, axis=1, keepdims=True)
        yc = y - mean
        var = jnp.mean(yc * yc, axis=1, keepdims=True)
        yn = yc * lax.rsqrt(var + RW_GN_EPS) * lg_ref[...] + lb_ref[...]
        k = k_ref[...]
        ka = ka_ref[...]
        ksum = k * (1.0 + (ao_ref[...] - 1.0) * ka) + k * (1.0 + (a_ref[...] - 1.0) * ka)
        bonus = jnp.sum(r_ref[...] * ksum * rk_ref[...], axis=1, keepdims=True) * v_ref[...]
        y_ref[...] = yn + bonus


def _rw_scan_call(r, k, v, w, a, kk_tab, ka_tab, ncc, reverse, fin=()):
    T, K, BH = r.shape
    tc = SCAN_TC
    NC = T // tc

    def cidx(c):
        return jnp.where(c < ncc, ncc - 1 - c, NC - 1 - (c - ncc)) if reverse else c

    sh_spec = pl.BlockSpec((tc, K, BH), lambda c: (cidx(c), 0, 0))
    tab = _const_spec((K, BH))
    return pl.pallas_call(
        functools.partial(_rw_scan_body, tc=tc, reverse=reverse, finish=bool(fin)),
        grid=(NC,),
        in_specs=[sh_spec] * 5 + [tab, tab] + ([sh_spec, sh_spec, tab, tab, tab] if fin else []),
        out_specs=sh_spec,
        out_shape=jax.ShapeDtypeStruct((T, K, BH), F32),
        scratch_shapes=[pltpu.VMEM((K, K, BH), F32)],
        compiler_params=_cparams(1),
        name="rw_scan_bwd" if reverse else "rw_scan_fwd",
    )(r, k, v, w, a, kk_tab, ka_tab, *fin)


def _rope_tables(n_ctx, n_lat, rot_dim):
    mfreq = rot_dim // 4
    t = jnp.arange(n_lat)
    row = (t // GRID_W).astype(F32)
    col = (t % GRID_W).astype(F32)
    inv = ROPE_BASE ** (-jnp.arange(mfreq, dtype=F32) / mfreq)
    ar = row[:, None] * inv
    ac = col[:, None] * inv
    cos = jnp.concatenate([jnp.cos(ar), jnp.cos(ar), jnp.cos(ac), jnp.cos(ac)], axis=1)
    sin = jnp.concatenate([-jnp.sin(ar), jnp.sin(ar), -jnp.sin(ac), jnp.sin(ac)], axis=1)
    cos = jnp.concatenate([jnp.ones((n_ctx, rot_dim), F32), cos], axis=0)
    sin = jnp.concatenate([jnp.zeros((n_ctx, rot_dim), F32), sin], axis=0)
    return cos, sin


def _swap_perm(n, half):
    idx = jnp.arange(n)
    return idx ^ half


def _to_scan(x, B):
    lead = x.shape[:-3]
    T, D = x.shape[-2:]
    H = D // RW_HEAD
    x = x.reshape(lead + (B, T, H, RW_HEAD))
    nl = len(lead)
    perm = tuple(range(nl)) + (nl + 1, nl + 3, nl, nl + 2)
    return jnp.transpose(x, perm).reshape(lead + (T, RW_HEAD, B * H))


def _head_tab(p, B):
    H = p.size // RW_HEAD
    t = p.reshape(H, RW_HEAD).T
    return jnp.tile(t[:, None, :], (1, B, 1)).reshape(RW_HEAD, B * H).astype(F32)


def kernel(x, c, ctx, c_ctx, ada_w, ada_b, norm1_g, norm2_g, ffn_w1, ffn_w3, ffn_w2, final_norm_g, da_wqkv, da_lambda, da_subln_g, da_wo, rw_mu, rw_wrkv, rw_w0, rw_w1, rw_w2, rw_a0, rw_a1, rw_a2, rw_g1, rw_g2, rw_k_k, rw_k_a, rw_r_k, rw_lnx_g, rw_lnx_b, rw_wo, mla_wdown, mla_q_norm_g, mla_wuq, mla_kv_norm_g, mla_wukv, mla_wo):
    B, N, D = x.shape
    n_ctx = ctx.shape[1]
    depth = ada_w.shape[0]
    assert n_ctx % TM == 0 and N % TM == 0 and n_ctx % SCAN_TC == 0
    nct = n_ctx // TM
    T = n_ctx + N
    assert (T // TM) % KV_TILES == 0 and (T // TM // KV_TILES) % 2 == 1

    xa = jnp.concatenate([ctx, x], axis=1)
    R = -(-(B + 1) // 8) * 8
    cond = jnp.zeros((R, D), F32).at[:B].set(c).at[B].set(c_ctx)
    mod = _ada_call(cond, ada_w, ada_b).reshape(depth, R, 6, D)

    da_cos, da_sin = _rope_tables(n_ctx, N, DA_HEAD_DIM)
    da_scale = (DA_HEAD_DIM ** -0.5) * LOG2E
    da_cq, da_sq = (da_cos * da_scale).T, (da_sin * da_scale).T
    da_ck, da_sk = jnp.tile(da_cos, (1, 2)), jnp.tile(da_sin, (1, 2))
    ml_cos, ml_sin = _rope_tables(n_ctx, N, MLA_ROPE)
    ml_scale = ((MLA_NOPE + MLA_ROPE) ** -0.5) * LOG2E
    ml_cq, ml_sq = jnp.tile((ml_cos * ml_scale).T, (2, 1)), jnp.tile((ml_sin * ml_scale).T, (2, 1))
    zpad = jnp.zeros((T, 64), F32)
    ml_ck = jnp.concatenate([ml_cos, ml_cos, zpad], axis=1)
    ml_sk = jnp.concatenate([ml_sin, ml_sin, zpad], axis=1)

    for i in range(depth):
        kind, j = i % N_MIXERS, i // N_MIXERS
        mod_i = mod[i]
        g1 = norm1_g[i].reshape(1, D)
        gate = None
        if kind == 0:
            wq, wk, wv = jnp.split(da_wqkv[j], 3, axis=1)
            qt, kk, vt = _da_pre_call(xa, mod_i, g1, wq.T.astype(BF16), wk.astype(BF16),
                                      wv.T.astype(BF16), da_cq, da_sq, da_ck, da_sk, nct)
            lam_init = 0.8 - 0.6 * math.exp(-0.3 * i)
            o = _attn_call(da_lambda[j], da_subln_g[j].reshape(1, -1), qt, kk, vt,
                           mode="diff", nct=nct, lam_init=lam_init)
            wo = da_wo[j]
        elif kind == 1:
            H = D // RW_HEAD
            w1c = jnp.concatenate([rw_w1[j, 0], rw_w1[j, 1]], axis=1).astype(BF16)
            a1c = jnp.concatenate([rw_a1[j, 0], rw_a1[j, 1]], axis=1).astype(BF16)
            zl = jnp.zeros_like(rw_w2[j, 0])
            w2p = jnp.stack([jnp.concatenate([rw_w2[j, 0], zl], axis=0),
                             jnp.concatenate([zl, rw_w2[j, 1]], axis=0)]).astype(BF16)
            a2p = jnp.stack([jnp.concatenate([rw_a2[j, 0], zl], axis=0),
                             jnp.concatenate([zl, rw_a2[j, 1]], axis=0)]).astype(BF16)
            gl = rw_g1.shape[2]
            glp = -(-gl // 128) * 128
            g1p = jnp.pad(rw_g1[j], ((0, 0), (0, glp - gl))).astype(BF16)
            g2p = jnp.pad(rw_g2[j], ((0, glp - gl), (0, 0))).astype(BF16)
            *feats, gate = _rw_pre_call(
                xa, mod_i, g1, rw_mu[j], rw_wrkv[j, 0].astype(BF16), rw_wrkv[j, 1].astype(BF16),
                rw_wrkv[j, 2].astype(BF16), w1c, w2p, rw_w0[j], a1c, a2p, rw_a0[j], g1p, g2p, nct)
            rs, ks, vs, wf, wb, af, ab = (_to_scan(u, B) for u in feats)
            ka_tab = _head_tab(rw_k_a[j], B)
            kk_tab = _head_tab(rw_k_k[j], B)
            y0 = _rw_scan_call(rs, ks, vs, wf, af, kk_tab, ka_tab, n_ctx // SCAN_TC, False)
            z = _rw_scan_call(rs, ks, vs, wb, ab, kk_tab, ka_tab, n_ctx // SCAN_TC, True,
                              fin=(y0, af, _head_tab(rw_r_k[j], B), _head_tab(rw_lnx_g[j], B),
                                   _head_tab(rw_lnx_b[j], B)))
            o = jnp.transpose(z.reshape(T, RW_HEAD, B, H), (2, 0, 3, 1)).reshape(B, T, D)
            wo = rw_wo[j]
        else:
            wd = mla_wdown[j]
            nlat = MLA_Q_LORA + MLA_KV_LORA
            kr = wd[:, nlat:]
            krs = kr[:, _swap_perm(MLA_ROPE, 8)]
            zc = jnp.zeros((D, 64), F32)
            wext = jnp.concatenate([wd[:, :nlat], kr, kr, zc, krs, krs, zc], axis=1).astype(BF16)
            wuq = mla_wuq[j].reshape(MLA_Q_LORA, MLA_HEADS // 2, 2, MLA_NOPE + MLA_ROPE)
            qn = wuq[..., :MLA_NOPE].reshape(MLA_Q_LORA, MLA_HEADS // 2, 2 * MLA_NOPE)
            qr = wuq[..., MLA_NOPE:].reshape(MLA_Q_LORA, MLA_HEADS // 2, 2 * MLA_ROPE)
            qz = jnp.zeros((MLA_Q_LORA, MLA_HEADS // 2, 64), F32)
            wuqt = jnp.concatenate([qn, qr, qz], axis=2).reshape(MLA_Q_LORA, -1).T.astype(BF16)
            wukv = mla_wukv[j].reshape(MLA_KV_LORA, MLA_HEADS, MLA_NOPE + MLA_V)
            wkn = wukv[..., :MLA_NOPE].reshape(MLA_KV_LORA, -1).astype(BF16)
            wvt = wukv[..., MLA_NOPE:].reshape(MLA_KV_LORA, -1).T.astype(BF16)
            qt, kk, vt = _mla_pre_call(xa, mod_i, g1, wext, mla_q_norm_g[j].reshape(1, -1),
                                       mla_kv_norm_g[j].reshape(1, -1), wuqt, wkn, wvt,
                                       ml_cq, ml_sq, ml_ck, ml_sk, nct, ml_scale)
            o = _attn_call(jnp.zeros((4, 64), F32), jnp.ones((1, 128), F32), qt, kk, vt,
                           mode="mla", nct=nct, lam_init=0.0)
            wo = mla_wo[j]
        xa = _post_call(xa, o, gate, mod_i, wo.astype(BF16), norm2_g[i].reshape(1, D),
                        ffn_w1[i].astype(BF16), ffn_w3[i].astype(BF16), ffn_w2[i].astype(BF16), nct,
                        final_g=final_norm_g.reshape(1, D) if i == depth - 1 else None)
    return xa
```

```python
import functools
import math

import jax
import jax.numpy as jnp
from jax import lax
from jax.experimental import pallas as pl
from jax.experimental.pallas import tpu as pltpu

F32 = jnp.float32
BF16 = jnp.bfloat16

GRID_W = 64
ROPE_BASE = 10000.0
NORM_EPS = 1e-6
N_MIXERS = 3
DA_HEAD_DIM = 64
DA_SUBLN_EPS = 1e-5
RW_HEAD = 64
RW_GN_EPS = 64e-5
MLA_HEADS = 16
MLA_Q_LORA = 256
MLA_KV_LORA = 128
MLA_NOPE = 64
MLA_ROPE = 32
MLA_V = 64
LOG2E = math.log2(math.e)

TM = 256
KV_TILES = 3
PAIR_UNROLL = 5
SCAN_TC = 32
ADA_TN = 1536
VMEM_LIMIT = 56 * 1024 * 1024


def _cparams(n_grid):
    return pltpu.CompilerParams(dimension_semantics=("arbitrary",) * n_grid, vmem_limit_bytes=VMEM_LIMIT)


def _const_spec(shape):
    nd = len(shape)
    return pl.BlockSpec(shape, lambda *_: (0,) * nd)


def _norm_mod(x, g, shift, scale):
    ms = jnp.mean(x * x, axis=-1, keepdims=True)
    return (x * lax.rsqrt(ms + NORM_EPS) * g) * (1.0 + scale) + shift


def _dot(a, b):
    return jnp.dot(a, b, preferred_element_type=F32)


def _dot_nt(a, b):
    return lax.dot_general(a, b, (((1,), (1,)), ((), ())), preferred_element_type=F32)


def _ada_body(c_ref, w_ref, b_ref, o_ref):
    c = c_ref[...]
    s = (c * jax.nn.sigmoid(c)).astype(BF16)
    o_ref[0] = _dot(s, w_ref[0].astype(BF16)) + b_ref[0]


def _ada_call(cond, ada_w, ada_b):
    L, D, D6 = ada_w.shape
    R = cond.shape[0]
    return pl.pallas_call(
        _ada_body,
        grid=(L, D6 // ADA_TN),
        in_specs=[
            pl.BlockSpec((R, D), lambda l, j: (0, 0)),
            pl.BlockSpec((1, D, ADA_TN), lambda l, j: (l, 0, j)),
            pl.BlockSpec((1, 1, ADA_TN), lambda l, j: (l, 0, j)),
        ],
        out_specs=pl.BlockSpec((1, R, ADA_TN), lambda l, j: (l, 0, j)),
        out_shape=jax.ShapeDtypeStruct((L, R, D6), F32),
        compiler_params=_cparams(2),
        name="ada_mod",
    )(cond, ada_w, ada_b.reshape(L, 1, D6))


def _tok_specs(B, nct, D):
    x_spec = pl.BlockSpec((1, TM, D), lambda b, t: (b, t, 0))
    mod_spec = pl.BlockSpec((1, 6, D), lambda b, t: (jnp.where(t < nct, B, b), 0, 0))
    return x_spec, mod_spec


def _da_pre_body(x_ref, mod_ref, g_ref, wqt_ref, wk_ref, wvt_ref, cq_ref, sq_ref, ck_ref, sk_ref,
                 qt_ref, k_ref, vt_ref):
    m = mod_ref[0]
    hb = _norm_mod(x_ref[0], g_ref[...], m[0:1], m[1:2]).astype(BF16)
    D = hb.shape[1]
    qt = _dot_nt(wqt_ref[...], hb)
    cq = cq_ref[...]
    sq = sq_ref[...]
    hd = DA_HEAD_DIM
    for h in range(D // hd):
        blk = [qt[h * hd + 16 * i:h * hd + 16 * (i + 1)] for i in range(4)]
        for i in range(4):
            r0 = 16 * i
            out = blk[i] * cq[r0:r0 + 16] + blk[i ^ 1] * sq[r0:r0 + 16]
            qt_ref[0, 0, h * hd + r0:h * hd + r0 + 16, :] = out.astype(BF16)
    k = _dot(hb, wk_ref[...])
    ck = ck_ref[...]
    sk = sk_ref[...]
    lane = lax.broadcasted_iota(jnp.int32, ck.shape, 1)
    first_half = (lane & 16) == 0
    for j in range(D // 128):
        kj = k[:, 128 * j:128 * (j + 1)]
        ks = jnp.where(first_half, pltpu.roll(kj, 112, axis=1), pltpu.roll(kj, 16, axis=1))
        k_ref[0, :, 128 * j:128 * (j + 1)] = (kj * ck + ks * sk).astype(BF16)
    vt_ref[0, 0] = _dot_nt(wvt_ref[...], hb).astype(BF16)


def _da_pre_call(xa, mod_i, g1, wqt, wk, wvt, cq, sq, ck, sk, nct):
    B, T, D = xa.shape
    NT = T // TM
    x_spec, mod_spec = _tok_specs(B, nct, D)
    fm_spec = pl.BlockSpec((1, 1, D, TM), lambda b, t: (b, t, 0, 0))
    return pl.pallas_call(
        _da_pre_body,
        grid=(B, NT),
        in_specs=[
            x_spec, mod_spec, _const_spec((1, D)),
            _const_spec((D, D)), _const_spec((D, D)), _const_spec((D, D)),
            pl.BlockSpec((64, TM), lambda b, t: (0, t)), pl.BlockSpec((64, TM), lambda b, t: (0, t)),
            pl.BlockSpec((TM, 128), lambda b, t: (t, 0)), pl.BlockSpec((TM, 128), lambda b, t: (t, 0)),
        ],
        out_specs=[fm_spec, x_spec, fm_spec],
        out_shape=[
            jax.ShapeDtypeStruct((B, NT, D, TM), BF16),
            jax.ShapeDtypeStruct((B, T, D), BF16),
            jax.ShapeDtypeStruct((B, NT, D, TM), BF16),
        ],
        compiler_params=_cparams(2),
        name="da_pre",
    )(xa, mod_i, g1, wqt, wk, wvt, cq, sq, ck, sk)


def _attn_body(lam_ref, g_ref, q_ref, k_ref, v_ref, o_ref, m_ref, l_ref, acc_ref, s_ref, mx_ref, q2_ref,
               *, mode, nct, nt, lam_init, kvt, unroll):
    dk, tq = q_ref.shape[2], q_ref.shape[3]

    def prep_q(qb):
        q = q_ref[0, qb].astype(F32)
        rows = lax.broadcasted_iota(jnp.int32, (dk, tq), 0)
        if mode == "diff":
            sel_a = rows < 64
            qa = jnp.where(sel_a, q, 0.0)
            qb_ = jnp.where(sel_a, 0.0, q)
        else:
            blk = rows // 32
            in_a = (blk == 0) | (blk == 1) | (blk == 4)
            in_b = (blk == 2) | (blk == 3) | (blk == 5)
            qa = jnp.where(in_a, q, 0.0)
            qb_ = jnp.where(in_b, q, 0.0)
        q2_ref[...] = jnp.concatenate([qa, qb_], axis=1).astype(BF16)

    def init():
        m_ref[...] = jnp.full(m_ref.shape, -1e30, F32)
        l_ref[...] = jnp.zeros(l_ref.shape, F32)
        acc_ref[...] = jnp.zeros(acc_ref.shape, F32)

    def qk(c0, ntile, slot):
        tk = ntile * TM
        k0 = c0 * TM if isinstance(c0, int) else pl.multiple_of(c0 * TM, TM)
        s = _dot(k_ref[0, pl.ds(k0, tk), :], q2_ref[...])
        s_ref[slot, 0:tk, 0:2 * tq] = s
        mx_ref[slot] = jnp.max(s, axis=0, keepdims=True)

    def softmax_pv(c0, ntile, slot):
        s = s_ref[slot, 0:ntile * TM, 0:2 * tq]
        m_prev = m_ref[...]
        m_new = jnp.maximum(m_prev, mx_ref[slot])
        alpha = jnp.exp2(m_prev - m_new)
        p = jnp.exp2(s - m_new)
        l_ref[...] = alpha * l_ref[...] + jnp.sum(p, axis=0, keepdims=True)
        pb = p.astype(BF16)
        vc = jnp.concatenate([v_ref[0, c0 + j] for j in range(ntile)], axis=1)
        acc_ref[...] = alpha * acc_ref[...] + _dot(vc, pb)
        m_ref[...] = m_new

    def finish(qb):
        a = acc_ref[...] * (1.0 / l_ref[...])
        if mode == "diff":
            lv = lam_ref[...]
            lam = (jnp.exp(jnp.sum(lv[0:1] * lv[1:2], keepdims=True))
                   - jnp.exp(jnp.sum(lv[2:3] * lv[3:4], keepdims=True)) + lam_init)
            o = a[:, :tq] - lam * a[:, tq:]
            ms = jnp.mean(o * o, axis=0, keepdims=True)
            o = o * lax.rsqrt(ms + DA_SUBLN_EPS)
            ot = (o.T * g_ref[...]) * (1.0 - lam_init)
        else:
            ot = jnp.concatenate([a[:64, :tq], a[64:, tq:]], axis=0).T
        if isinstance(qb, int):
            o_ref[0, qb * TM:(qb + 1) * TM, :] = ot.astype(BF16)
        else:
            o_ref[0, pl.ds(pl.multiple_of(qb * TM, TM), TM), :] = ot.astype(BF16)

    for qb in range(nct):
        prep_q(qb)
        init()
        for c in range(nct):
            qk(c, 1, 0)
            softmax_pv(c, 1, 0)
        finish(qb)

    n = nt // kvt
    nq = nt - nct
    last = (n - 1) * kvt
    npairs = (n - 1) // 2
    if npairs % unroll:
        unroll = 1

    def tile_core(slot0):
        init()

        def pairs(j, carry):
            for u in range(unroll):
                c = 2 * (j * unroll + u) * kvt
                qk(c + kvt, kvt, 1 - slot0)
                softmax_pv(c, kvt, slot0)
                qk(c + 2 * kvt, kvt, slot0)
                softmax_pv(c + kvt, kvt, 1 - slot0)
            return carry

        lax.fori_loop(0, npairs // unroll + jnp.minimum(pl.program_id(0), 0), pairs, 0)

    def tile_end(qb, slot0, qb_next):
        prep_q(qb_next)
        qk(0, kvt, 1 - slot0)
        softmax_pv(last, kvt, slot0)
        finish(qb)

    prep_q(nct)
    qk(0, kvt, 0)

    def two_tiles(i, carry):
        qb = nct + 2 * i
        tile_core(0)
        tile_end(qb, 0, qb + 1)
        tile_core(1)
        tile_end(qb + 1, 1, jnp.minimum(qb + 2, nt - 1))
        return carry

    lax.fori_loop(0, nq // 2, two_tiles, 0)
    if nq % 2:
        tile_core(0)
        softmax_pv(last, kvt, 0)
        finish(nt - 1)


def _attn_call(lam, g, qt, k, vt, *, mode, nct, lam_init, kvt=KV_TILES, unroll=PAIR_UNROLL):
    B, NT, GD, _ = qt.shape
    T = NT * TM
    G = vt.shape[2] // 128
    dk = GD // G
    body = functools.partial(_attn_body, mode=mode, nct=nct, nt=NT, lam_init=lam_init, kvt=kvt, unroll=unroll)
    return pl.pallas_call(
        body,
        grid=(B, G),
        in_specs=[
            _const_spec(lam.shape), _const_spec(g.shape),
            pl.BlockSpec((1, NT, dk, TM), lambda b, h: (b, 0, h, 0)),
            pl.BlockSpec((1, T, dk), lambda b, h: (b, 0, h)),
            pl.BlockSpec((1, NT, 128, TM), lambda b, h: (b, 0, h, 0)),
        ],
        out_specs=pl.BlockSpec((1, T, 128), lambda b, h: (b, 0, h)),
        out_shape=jax.ShapeDtypeStruct((B, T, G * 128), BF16),
        scratch_shapes=[
            pltpu.VMEM((1, 2 * TM), F32), pltpu.VMEM((1, 2 * TM), F32), pltpu.VMEM((128, 2 * TM), F32),
            pltpu.VMEM((2, kvt * TM, 2 * TM), F32), pltpu.VMEM((2, 1, 2 * TM), F32),
            pltpu.VMEM((dk, 2 * TM), BF16),
        ],
        compiler_params=_cparams(2),
        name="attn_" + mode,
    )(lam, g, qt, k, vt)


def _post_body(*refs, has_gate, has_final):
    x_ref, o_ref = refs[:2]
    nxt = 3 if has_gate else 2
    o = (o_ref[0] * refs[2][0]).astype(BF16) if has_gate else o_ref[0]
    mod_ref, wo_ref, n2_ref, w1_ref, w3_ref, w2_ref = refs[nxt:nxt + 6]
    out_ref = refs[-1]
    m = mod_ref[0]
    x1 = x_ref[0] + m[2:3] * _dot(o, wo_ref[...])
    h = _norm_mod(x1, n2_ref[...], m[3:4], m[4:5]).astype(BF16)
    a = _dot(h, w1_ref[...])
    b = _dot(h, w3_ref[...])
    u = (a * jax.nn.sigmoid(a) * b).astype(BF16)
    x2 = x1 + m[5:6] * _dot(u, w2_ref[...])
    if has_final:
        ms = jnp.mean(x2 * x2, axis=-1, keepdims=True)
        x2 = x2 * lax.rsqrt(ms + NORM_EPS) * refs[nxt + 6][...]
    out_ref[0] = x2


def _post_call(xa, o, gate, mod_i, wo, n2, w1, w3, w2, nct, final_g=None):
    B, T, D = xa.shape
    F = w1.shape[1]
    has_gate = gate is not None
    has_final = final_g is not None
    if has_final:
        x_spec = pl.BlockSpec((1, TM, D), lambda b, t: (b, t + nct, 0))
        mod_spec = pl.BlockSpec((1, 6, D), lambda b, t: (b, 0, 0))
    else:
        x_spec, mod_spec = _tok_specs(B, nct, D)
    ins = [xa, o] + ([gate] if has_gate else []) + [mod_i, wo, n2, w1, w3, w2] + ([final_g] if has_final else [])
    one = pl.Buffered(1)
    specs = [x_spec, x_spec] + ([x_spec] if has_gate else []) + [
        mod_spec,
        pl.BlockSpec((D, D), lambda b, t: (0, 0), pipeline_mode=one),
        _const_spec((1, D)),
        pl.BlockSpec((D, F), lambda b, t: (0, 0), pipeline_mode=one),
        pl.BlockSpec((D, F), lambda b, t: (0, 0), pipeline_mode=one),
        pl.BlockSpec((F, D), lambda b, t: (0, 0), pipeline_mode=one),
    ] + ([_const_spec((1, D))] if has_final else [])
    nt = T // TM - (nct if has_final else 0)
    out_spec = pl.BlockSpec((1, TM, D), lambda b, t: (b, t, 0))
    out_shape = jax.ShapeDtypeStruct((B, nt * TM, D), F32)
    return pl.pallas_call(
        functools.partial(_post_body, has_gate=has_gate, has_final=has_final),
        grid=(B, nt),
        in_specs=specs,
        out_specs=out_spec,
        out_shape=out_shape,
        input_output_aliases={} if has_final else {0: 0},
        compiler_params=_cparams(2),
        name="post_ffn_final" if has_final else "post_ffn",
    )(*ins)


def _mla_pre_body(x_ref, mod_ref, g_ref, wext_ref, gq_ref, gkv_ref, wuqt_ref, wkn_ref, wvt_ref,
                  cq_ref, sq_ref, ck_ref, sk_ref, qt_ref, k_ref, vt_ref, *, scale):
    m = mod_ref[0]
    hb = _norm_mod(x_ref[0], g_ref[...], m[0:1], m[1:2]).astype(BF16)
    down = _dot(hb, wext_ref[...])
    ql = down[:, :MLA_Q_LORA]
    cq_lat = (ql * lax.rsqrt(jnp.mean(ql * ql, axis=-1, keepdims=True) + NORM_EPS) * gq_ref[...]).astype(BF16)
    kl = down[:, MLA_Q_LORA:MLA_Q_LORA + MLA_KV_LORA]
    ckv = (kl * lax.rsqrt(jnp.mean(kl * kl, axis=-1, keepdims=True) + NORM_EPS) * gkv_ref[...]).astype(BF16)
    kr = (down[:, 384:512] * ck_ref[...] + down[:, 512:640] * sk_ref[...]).astype(BF16)
    kn = _dot(ckv, wkn_ref[...])
    npair = kn.shape[1] // 128
    for p in range(npair):
        k_ref[0, :, 256 * p:256 * p + 128] = kn[:, 128 * p:128 * (p + 1)].astype(BF16)
        k_ref[0, :, 256 * p + 128:256 * (p + 1)] = kr
    vt_ref[0, 0] = _dot_nt(wvt_ref[...], ckv).astype(BF16)
    qt = _dot_nt(wuqt_ref[...], cq_lat)
    cq = cq_ref[...]
    sq = sq_ref[...]
    for p in range(npair):
        base = 256 * p
        qt_ref[0, 0, base:base + 128, :] = (qt[base:base + 128] * scale).astype(BF16)
        blk = [qt[base + 128 + 8 * i:base + 136 + 8 * i] for i in range(8)]
        for i in range(8):
            out = blk[i] * cq[8 * i:8 * i + 8] + blk[i ^ 1] * sq[8 * i:8 * i + 8]
            qt_ref[0, 0, base + 128 + 8 * i:base + 136 + 8 * i, :] = out.astype(BF16)
        qt_ref[0, 0, base + 192:base + 256, :] = jnp.zeros((64, qt.shape[1]), BF16)


def _mla_pre_call(xa, mod_i, g1, wext, gq, gkv, wuqt, wkn, wvt, cq, sq, ck, sk, nct, scale):
    B, T, D = xa.shape
    NT = T // TM
    x_spec, mod_spec = _tok_specs(B, nct, D)
    GD = wuqt.shape[0]
    return pl.pallas_call(
        functools.partial(_mla_pre_body, scale=scale),
        grid=(B, NT),
        in_specs=[
            x_spec, mod_spec, _const_spec((1, D)),
            _const_spec(wext.shape), _const_spec(gq.shape), _const_spec(gkv.shape),
            _const_spec(wuqt.shape), _const_spec(wkn.shape), _const_spec(wvt.shape),
            pl.BlockSpec((64, TM), lambda b, t: (0, t)), pl.BlockSpec((64, TM), lambda b, t: (0, t)),
            pl.BlockSpec((TM, 128), lambda b, t: (t, 0)), pl.BlockSpec((TM, 128), lambda b, t: (t, 0)),
        ],
        out_specs=[
            pl.BlockSpec((1, 1, GD, TM), lambda b, t: (b, t, 0, 0)),
            pl.BlockSpec((1, TM, GD), lambda b, t: (b, t, 0)),
            pl.BlockSpec((1, 1, D, TM), lambda b, t: (b, t, 0, 0)),
        ],
        out_shape=[
            jax.ShapeDtypeStruct((B, NT, GD, TM), BF16),
            jax.ShapeDtypeStruct((B, T, GD), BF16),
            jax.ShapeDtypeStruct((B, NT, D, TM), BF16),
        ],
        compiler_params=_cparams(2),
        name="mla_pre",
    )(xa, mod_i, g1, wext, gq, gkv, wuqt, wkn, wvt, cq, sq, ck, sk)


def _rw_pre_body(x_ref, xp_ref, xn_ref, mod_ref, g_ref, mu_ref, wr_ref, wk_ref, wv_ref, w1_ref, w2_ref, w0_ref,
                 a1_ref, a2_ref, a0_ref, g1_ref, g2_ref, r_ref, k_ref, v_ref, wf_ref, wb_ref, af_ref, ab_ref, gt_ref,
                 *, nct, nt):
    t = pl.program_id(1)
    m = mod_ref[0]
    g = g_ref[...]
    h = _norm_mod(x_ref[0], g, m[0:1], m[1:2])
    hp = _norm_mod(xp_ref[0], g, m[0:1], m[1:2])[7:8]
    hn = _norm_mod(xn_ref[0], g, m[0:1], m[1:2])[0:1]
    has_prev = jnp.logical_and(t != 0, t != nct)
    has_next = jnp.logical_and(t != nct - 1, t != nt - 1)
    hp = jnp.where(has_prev, hp, 0.0)
    hn = jnp.where(has_next, hn, 0.0)
    tm = h.shape[0]
    rid = lax.broadcasted_iota(jnp.int32, h.shape, 0)
    h_m1 = jnp.where(rid == 0, hp, pltpu.roll(h, 1, axis=0))
    h_p1 = jnp.where(rid == tm - 1, hn, pltpu.roll(h, tm - 1, axis=0))
    cs = 0.5 * (h_m1 + h_p1) - h
    mu = mu_ref[...]

    def mix(s):
        return (h + cs * mu[s:s + 1]).astype(BF16)

    r_ref[0] = _dot(mix(0), wr_ref[...])
    k_ref[0] = _dot(mix(1), wk_ref[...])
    v_ref[0] = _dot(mix(2), wv_ref[...])
    lw = jnp.tanh(_dot(mix(3), w1_ref[...])).astype(BF16)
    la = _dot(mix(4), a1_ref[...]).astype(BF16)
    for z in range(2):
        wl = w0_ref[z:z + 1] + _dot(lw, w2_ref[z])
        (wf_ref, wb_ref)[z][0] = jnp.exp(-math.exp(-0.5) * jax.nn.sigmoid(wl))
        (af_ref, ab_ref)[z][0] = jax.nn.sigmoid(a0_ref[z:z + 1] + _dot(la, a2_ref[z]))
    gl = jax.nn.sigmoid(_dot(mix(5), g1_ref[...])).astype(BF16)
    gt_ref[0] = _dot(gl, g2_ref[...])


def _rw_pre_call(xa, mod_i, g1n, mu, wr, wk, wv, w1, w2, w0, a1, a2, a0, g1, g2, nct):
    B, T, D = xa.shape
    NT = T // TM
    x_spec, mod_spec = _tok_specs(B, nct, D)
    r8 = TM // 8
    prev_spec = pl.BlockSpec((1, 8, D), lambda b, t: (b, jnp.maximum(t * r8 - 1, 0), 0))
    next_spec = pl.BlockSpec((1, 8, D), lambda b, t: (b, jnp.minimum((t + 1) * r8, T // 8 - 1), 0))
    consts = [g1n, mu, wr, wk, wv, w1, w2, w0, a1, a2, a0, g1, g2]
    tok = jax.ShapeDtypeStruct((B, T, D), F32)
    return pl.pallas_call(
        functools.partial(_rw_pre_body, nct=nct, nt=NT),
        grid=(B, NT),
        in_specs=[x_spec, prev_spec, next_spec, mod_spec] + [_const_spec(c.shape) for c in consts],
        out_specs=[x_spec] * 8,
        out_shape=[tok] * 8,
        compiler_params=_cparams(2),
        name="rw_pre",
    )(xa, xa, xa, mod_i, *consts)


def _rw_scan_body(*refs, tc, reverse, finish):
    r_ref, k_ref, v_ref, w_ref, a_ref, kk_ref, ka_ref = refs[:7]
    y_ref, s_ref = refs[-2:]

    @pl.when(pl.program_id(0) == 0)
    def _():
        s_ref[...] = jnp.zeros(s_ref.shape, F32)

    nv, nk, bh = s_ref.shape
    rowid = lax.broadcasted_iota(jnp.int32, (8, bh), 0)
    low4, low2, low1 = (rowid & 4) == 0, (rowid & 2) == 0, (rowid & 1) == 0

    def fold(x, y, keep, dist):
        u = jnp.where(keep, x, y)
        w = jnp.where(keep, y, x)
        if dist == 4:
            return u + pltpu.roll(w, 4, axis=0)
        return u + jnp.where(keep, pltpu.roll(w, 8 - dist, axis=0), pltpu.roll(w, dist, axis=0))

    def row_sums(parts):
        z = [fold(parts[i], parts[i + 4], low4, 4) for i in range(4)]
        y = [fold(z[i], z[i + 2], low2, 2) for i in range(2)]
        return fold(y[0], y[1], low1, 1)

    def step(i, p):
        te = tc - 1 - i if reverse else i
        r = r_ref[te]
        k = k_ref[te]
        a = a_ref[te]
        kk = k * kk_ref[...]
        nrm = jnp.sqrt(jnp.sum(kk * kk, axis=0, keepdims=True))
        kk = kk / jnp.maximum(nrm, 1e-12)
        pn = p * w_ref[te]
        ip = 1.0 / pn
        at = -kk * p
        bt = kk * a * ip
        kt = k * (1.0 + (a - 1.0) * ka_ref[...]) * ip
        rt = r * pn
        for g in range(nv // 8):
            parts = []
            for j in range(8):
                vi = 8 * g + j
                s = s_ref[vi]
                sa = jnp.sum(s * at, axis=0, keepdims=True)
                vv = v_ref[te, pl.ds(vi, 1), :]
                sn = s + sa * bt + vv * kt
                s_ref[vi] = sn
                parts.append(jnp.sum((sn * rt).reshape(nk // 8, 8, bh), axis=0))
            y_ref[te, 8 * g:8 * g + 8, :] = row_sums(parts)
        return pn

    p_end = lax.fori_loop(0, tc, step, jnp.ones((nk, bh), F32))
    for vi in range(nv):
        s_ref[vi] = s_ref[vi] * p_end

    if finish:
        yo_ref, ao_ref, rk_ref, lg_ref, lb_ref = refs[7:12]
        y = yo_ref[...] + y_ref[...]
        mean = jnp.mean(y, axis=1, keepdims=True)
        yc = y - mean
        var = jnp.mean(yc * yc, axis=1, keepdims=True)
        yn = yc * lax.rsqrt(var + RW_GN_EPS) * lg_ref[...] + lb_ref[...]
        k = k_ref[...]
        ka = ka_ref[...]
        ksum = k * (1.0 + (ao_ref[...] - 1.0) * ka) + k * (1.0 + (a_ref[...] - 1.0) * ka)
        bonus = jnp.sum(r_ref[...] * ksum * rk_ref[...], axis=1, keepdims=True) * v_ref[...]
        y_ref[...] = yn + bonus


def _rw_scan_call(r, k, v, w, a, kk_tab, ka_tab, ncc, reverse, fin=()):
    T, K, BH = r.shape
    tc = SCAN_TC
    NC = T // tc

    def cidx(c):
        return jnp.where(c < ncc, ncc - 1 - c, NC - 1 - (c - ncc)) if reverse else c

    sh_spec = pl.BlockSpec((tc, K, BH), lambda c: (cidx(c), 0, 0))
    tab = _const_spec((K, BH))
    return pl.pallas_call(
        functools.partial(_rw_scan_body, tc=tc, reverse=reverse, finish=bool(fin)),
        grid=(NC,),
        in_specs=[sh_spec] * 5 + [tab, tab] + ([sh_spec, sh_spec, tab, tab, tab] if fin else []),
        out_specs=sh_spec,
        out_shape=jax.ShapeDtypeStruct((T, K, BH), F32),
        scratch_shapes=[pltpu.VMEM((K, K, BH), F32)],
        compiler_params=_cparams(1),
        name="rw_scan_bwd" if reverse else "rw_scan_fwd",
    )(r, k, v, w, a, kk_tab, ka_tab, *fin)


def _rope_tables(n_ctx, n_lat, rot_dim):
    mfreq = rot_dim // 4
    t = jnp.arange(n_lat)
    row = (t // GRID_W).astype(F32)
    col = (t % GRID_W).astype(F32)
    inv = ROPE_BASE ** (-jnp.arange(mfreq, dtype=F32) / mfreq)
    ar = row[:, None] * inv
    ac = col[:, None] * inv
    cos = jnp.concatenate([jnp.cos(ar), jnp.cos(ar), jnp.cos(ac), jnp.cos(ac)], axis=1)
    sin = jnp.concatenate([-jnp.sin(ar), jnp.sin(ar), -jnp.sin(ac), jnp.sin(ac)], axis=1)
    cos = jnp.concatenate([jnp.ones((n_ctx, rot_dim), F32), cos], axis=0)
    sin = jnp.concatenate([jnp.zeros((n_ctx, rot_dim), F32), sin], axis=0)
    return cos, sin


def _swap_perm(n, half):
    idx = jnp.arange(n)
    return idx ^ half


def _to_scan(x, B):
    lead = x.shape[:-3]
    T, D = x.shape[-2:]
    H = D // RW_HEAD
    x = x.reshape(lead + (B, T, H, RW_HEAD))
    nl = len(lead)
    perm = tuple(range(nl)) + (nl + 1, nl + 3, nl, nl + 2)
    return jnp.transpose(x, perm).reshape(lead + (T, RW_HEAD, B * H))


def _head_tab(p, B):
    H = p.size // RW_HEAD
    t = p.reshape(H, RW_HEAD).T
    return jnp.tile(t[:, None, :], (1, B, 1)).reshape(RW_HEAD, B * H).astype(F32)


def kernel(x, c, ctx, c_ctx, ada_w, ada_b, norm1_g, norm2_g, ffn_w1, ffn_w3, ffn_w2, final_norm_g, da_wqkv, da_lambda, da_subln_g, da_wo, rw_mu, rw_wrkv, rw_w0, rw_w1, rw_w2, rw_a0, rw_a1, rw_a2, rw_g1, rw_g2, rw_k_k, rw_k_a, rw_r_k, rw_lnx_g, rw_lnx_b, rw_wo, mla_wdown, mla_q_norm_g, mla_wuq, mla_kv_norm_g, mla_wukv, mla_wo):
    B, N, D = x.shape
    n_ctx = ctx.shape[1]
    depth = ada_w.shape[0]
    assert n_ctx % TM == 0 and N % TM == 0 and n_ctx % SCAN_TC == 0
    nct = n_ctx // TM
    T = n_ctx + N
    assert (T // TM) % KV_TILES == 0 and (T // TM // KV_TILES) % 2 == 1

    xa = jnp.concatenate([ctx, x], axis=1)
    R = -(-(B + 1) // 8) * 8
    cond = jnp.zeros((R, D), F32).at[:B].set(c).at[B].set(c_ctx)
    mod = _ada_call(cond, ada_w, ada_b).reshape(depth, R, 6, D)

    da_cos, da_sin = _rope_tables(n_ctx, N, DA_HEAD_DIM)
    da_scale = (DA_HEAD_DIM ** -0.5) * LOG2E
    da_cq, da_sq = (da_cos * da_scale).T, (da_sin * da_scale).T
    da_ck, da_sk = jnp.tile(da_cos, (1, 2)), jnp.tile(da_sin, (1, 2))
    ml_cos, ml_sin = _rope_tables(n_ctx, N, MLA_ROPE)
    ml_scale = ((MLA_NOPE + MLA_ROPE) ** -0.5) * LOG2E
    ml_cq, ml_sq = jnp.tile((ml_cos * ml_scale).T, (2, 1)), jnp.tile((ml_sin * ml_scale).T, (2, 1))
    zpad = jnp.zeros((T, 64), F32)
    ml_ck = jnp.concatenate([ml_cos, ml_cos, zpad], axis=1)
    ml_sk = jnp.concatenate([ml_sin, ml_sin, zpad], axis=1)

    for i in range(depth):
        kind, j = i % N_MIXERS, i // N_MIXERS
        mod_i = mod[i]
        g1 = norm1_g[i].reshape(1, D)
        gate = None
        if kind == 0:
            wq, wk, wv = jnp.split(da_wqkv[j], 3, axis=1)
            qt, kk, vt = _da_pre_call(xa, mod_i, g1, wq.T.astype(BF16), wk.astype(BF16),
                                      wv.T.astype(BF16), da_cq, da_sq, da_ck, da_sk, nct)
            lam_init = 0.8 - 0.6 * math.exp(-0.3 * i)
            o = _attn_call(da_lambda[j], da_subln_g[j].reshape(1, -1), qt, kk, vt,
                           mode="diff", nct=nct, lam_init=lam_init)
            wo = da_wo[j]
        elif kind == 1:
            H = D // RW_HEAD
            w1c = jnp.concatenate([rw_w1[j, 0], rw_w1[j, 1]], axis=1).astype(BF16)
            a1c = jnp.concatenate([rw_a1[j, 0], rw_a1[j, 1]], axis=1).astype(BF16)
            zl = jnp.zeros_like(rw_w2[j, 0])
            w2p = jnp.stack([jnp.concatenate([rw_w2[j, 0], zl], axis=0),
                             jnp.concatenate([zl, rw_w2[j, 1]], axis=0)]).astype(BF16)
            a2p = jnp.stack([jnp.concatenate([rw_a2[j, 0], zl], axis=0),
                             jnp.concatenate([zl, rw_a2[j, 1]], axis=0)]).astype(BF16)
            gl = rw_g1.shape[2]
            glp = -(-gl // 128) * 128
            g1p = jnp.pad(rw_g1[j], ((0, 0), (0, glp - gl))).astype(BF16)
            g2p = jnp.pad(rw_g2[j], ((0, glp - gl), (0, 0))).astype(BF16)
            *feats, gate = _rw_pre_call(
                xa, mod_i, g1, rw_mu[j], rw_wrkv[j, 0].astype(BF16), rw_wrkv[j, 1].astype(BF16),
                rw_wrkv[j, 2].astype(BF16), w1c, w2p, rw_w0[j], a1c, a2p, rw_a0[j], g1p, g2p, nct)
            rs, ks, vs, wf, wb, af, ab = (_to_scan(u, B) for u in feats)
            ka_tab = _head_tab(rw_k_a[j], B)
            kk_tab = _head_tab(rw_k_k[j], B)
            y0 = _rw_scan_call(rs, ks, vs, wf, af, kk_tab, ka_tab, n_ctx // SCAN_TC, False)
            z = _rw_scan_call(rs, ks, vs, wb, ab, kk_tab, ka_tab, n_ctx // SCAN_TC, True,
                              fin=(y0, af, _head_tab(rw_r_k[j], B), _head_tab(rw_lnx_g[j], B),
                                   _head_tab(rw_lnx_b[j], B)))
            o = jnp.transpose(z.reshape(T, RW_HEAD, B, H), (2, 0, 3, 1)).reshape(B, T, D)
            wo = rw_wo[j]
        else:
            wd = mla_wdown[j]
            nlat = MLA_Q_LORA + MLA_KV_LORA
            kr = wd[:, nlat:]
            krs = kr[:, _swap_perm(MLA_ROPE, 8)]
            zc = jnp.zeros((D, 64), F32)
            wext = jnp.concatenate([wd[:, :nlat], kr, kr, zc, krs, krs, zc], axis=1).astype(BF16)
            wuq = mla_wuq[j].reshape(MLA_Q_LORA, MLA_HEADS // 2, 2, MLA_NOPE + MLA_ROPE)
            qn = wuq[..., :MLA_NOPE].reshape(MLA_Q_LORA, MLA_HEADS // 2, 2 * MLA_NOPE)
            qr = wuq[..., MLA_NOPE:].reshape(MLA_Q_LORA, MLA_HEADS // 2, 2 * MLA_ROPE)
            qz = jnp.zeros((MLA_Q_LORA, MLA_HEADS // 2, 64), F32)
            wuqt = jnp.concatenate([qn, qr, qz], axis=2).reshape(MLA_Q_LORA, -1).T.astype(BF16)
            wukv = mla_wukv[j].reshape(MLA_KV_LORA, MLA_HEADS, MLA_NOPE + MLA_V)
            wkn = wukv[..., :MLA_NOPE].reshape(MLA_KV_LORA, -1).astype(BF16)
            wvt = wukv[..., MLA_NOPE:].reshape(MLA_KV_LORA, -1).T.astype(BF16)
            qt, kk, vt = _mla_pre_call(xa, mod_i, g1, wext, mla_q_norm_g[j].reshape(1, -1),
                                       mla_kv_norm_g[j].reshape(1, -1), wuqt, wkn, wvt,
                                       ml_cq, ml_sq, ml_ck, ml_sk, nct, ml_scale)
            o = _attn_call(jnp.zeros((4, 64), F32), jnp.ones((1, 128), F32), qt, kk, vt,
                           mode="mla", nct=nct, lam_init=0.0)
            wo = mla_wo[j]
        xa = _post_call(xa, o, gate, mod_i, wo.astype(BF16), norm2_g[i].reshape(1, D),
                        ffn_w1[i].astype(BF16), ffn_w3[i].astype(BF16), ffn_w2[i].astype(BF16), nct,
                        final_g=final_norm_g.reshape(1, D) if i == depth - 1 else None)
    return xa
```

```python
import functools
import math

import jax
import jax.numpy as jnp
from jax import lax
from jax.experimental import pallas as pl
from jax.experimental.pallas import tpu as pltpu

F32 = jnp.float32
BF16 = jnp.bfloat16

GRID_W = 64
ROPE_BASE = 10000.0
NORM_EPS = 1e-6
N_MIXERS = 3
DA_HEAD_DIM = 64
DA_SUBLN_EPS = 1e-5
RW_HEAD = 64
RW_GN_EPS = 64e-5
MLA_HEADS = 16
MLA_Q_LORA = 256
MLA_KV_LORA = 128
MLA_NOPE = 64
MLA_ROPE = 32
MLA_V = 64
LOG2E = math.log2(math.e)

TM = 256
KV_TILES = 3
PAIR_UNROLL = 5
SCAN_TC = 64
ADA_TN = 1536
VMEM_LIMIT = 56 * 1024 * 1024


def _cparams(n_grid):
    return pltpu.CompilerParams(dimension_semantics=("arbitrary",) * n_grid, vmem_limit_bytes=VMEM_LIMIT)


def _const_spec(shape):
    nd = len(shape)
    return pl.BlockSpec(shape, lambda *_: (0,) * nd)


def _norm_mod(x, g, shift, scale):
    ms = jnp.mean(x * x, axis=-1, keepdims=True)
    return (x * lax.rsqrt(ms + NORM_EPS) * g) * (1.0 + scale) + shift


def _dot(a, b):
    return jnp.dot(a, b, preferred_element_type=F32)


def _dot_nt(a, b):
    return lax.dot_general(a, b, (((1,), (1,)), ((), ())), preferred_element_type=F32)


def _ada_body(c_ref, w_ref, b_ref, o_ref):
    c = c_ref[...]
    s = (c * jax.nn.sigmoid(c)).astype(BF16)
    o_ref[0] = _dot(s, w_ref[0].astype(BF16)) + b_ref[0]


def _ada_call(cond, ada_w, ada_b):
    L, D, D6 = ada_w.shape
    R = cond.shape[0]
    return pl.pallas_call(
        _ada_body,
        grid=(L, D6 // ADA_TN),
        in_specs=[
            pl.BlockSpec((R, D), lambda l, j: (0, 0)),
            pl.BlockSpec((1, D, ADA_TN), lambda l, j: (l, 0, j)),
            pl.BlockSpec((1, 1, ADA_TN), lambda l, j: (l, 0, j)),
        ],
        out_specs=pl.BlockSpec((1, R, ADA_TN), lambda l, j: (l, 0, j)),
        out_shape=jax.ShapeDtypeStruct((L, R, D6), F32),
        compiler_params=_cparams(2),
        name="ada_mod",
    )(cond, ada_w, ada_b.reshape(L, 1, D6))


def _tok_specs(B, nct, D):
    x_spec = pl.BlockSpec((1, TM, D), lambda b, t: (b, t, 0))
    mod_spec = pl.BlockSpec((1, 6, D), lambda b, t: (jnp.where(t < nct, B, b), 0, 0))
    return x_spec, mod_spec


def _da_pre_body(x_ref, mod_ref, g_ref, wqt_ref, wk_ref, wvt_ref, cq_ref, sq_ref, ck_ref, sk_ref,
                 qt_ref, k_ref, vt_ref):
    m = mod_ref[0]
    hb = _norm_mod(x_ref[0], g_ref[...], m[0:1], m[1:2]).astype(BF16)
    D = hb.shape[1]
    qt = _dot_nt(wqt_ref[...], hb)
    cq = cq_ref[...]
    sq = sq_ref[...]
    hd = DA_HEAD_DIM
    for h in range(D // hd):
        blk = [qt[h * hd + 16 * i:h * hd + 16 * (i + 1)] for i in range(4)]
        for i in range(4):
            r0 = 16 * i
            out = blk[i] * cq[r0:r0 + 16] + blk[i ^ 1] * sq[r0:r0 + 16]
            qt_ref[0, 0, h * hd + r0:h * hd + r0 + 16, :] = out.astype(BF16)
    k = _dot(hb, wk_ref[...])
    ck = ck_ref[...]
    sk = sk_ref[...]
    lane = lax.broadcasted_iota(jnp.int32, ck.shape, 1)
    first_half = (lane & 16) == 0
    for j in range(D // 128):
        kj = k[:, 128 * j:128 * (j + 1)]
        ks = jnp.where(first_half, pltpu.roll(kj, 112, axis=1), pltpu.roll(kj, 16, axis=1))
        k_ref[0, :, 128 * j:128 * (j + 1)] = (kj * ck + ks * sk).astype(BF16)
    vt_ref[0, 0] = _dot_nt(wvt_ref[...], hb).astype(BF16)


def _da_pre_call(xa, mod_i, g1, wqt, wk, wvt, cq, sq, ck, sk, nct):
    B, T, D = xa.shape
    NT = T // TM
    x_spec, mod_spec = _tok_specs(B, nct, D)
    fm_spec = pl.BlockSpec((1, 1, D, TM), lambda b, t: (b, t, 0, 0))
    return pl.pallas_call(
        _da_pre_body,
        grid=(B, NT),
        in_specs=[
            x_spec, mod_spec, _const_spec((1, D)),
            _const_spec((D, D)), _const_spec((D, D)), _const_spec((D, D)),
            pl.BlockSpec((64, TM), lambda b, t: (0, t)), pl.BlockSpec((64, TM), lambda b, t: (0, t)),
            pl.BlockSpec((TM, 128), lambda b, t: (t, 0)), pl.BlockSpec((TM, 128), lambda b, t: (t, 0)),
        ],
        out_specs=[fm_spec, x_spec, fm_spec],
        out_shape=[
            jax.ShapeDtypeStruct((B, NT, D, TM), BF16),
            jax.ShapeDtypeStruct((B, T, D), BF16),
            jax.ShapeDtypeStruct((B, NT, D, TM), BF16),
        ],
        compiler_params=_cparams(2),
        name="da_pre",
    )(xa, mod_i, g1, wqt, wk, wvt, cq, sq, ck, sk)


def _attn_body(lam_ref, g_ref, q_ref, k_ref, v_ref, o_ref, m_ref, l_ref, acc_ref, s_ref, mx_ref, q2_ref,
               *, mode, nct, nt, lam_init, kvt, unroll):
    dk, tq = q_ref.shape[2], q_ref.shape[3]

    def prep_q(qb):
        q = q_ref[0, qb].astype(F32)
        rows = lax.broadcasted_iota(jnp.int32, (dk, tq), 0)
        if mode == "diff":
            sel_a = rows < 64
            qa = jnp.where(sel_a, q, 0.0)
            qb_ = jnp.where(sel_a, 0.0, q)
        else:
            blk = rows // 32
            in_a = (blk == 0) | (blk == 1) | (blk == 4)
            in_b = (blk == 2) | (blk == 3) | (blk == 5)
            qa = jnp.where(in_a, q, 0.0)
            qb_ = jnp.where(in_b, q, 0.0)
        q2_ref[...] = jnp.concatenate([qa, qb_], axis=1).astype(BF16)

    def init():
        m_ref[...] = jnp.full(m_ref.shape, -1e30, F32)
        l_ref[...] = jnp.zeros(l_ref.shape, F32)
        acc_ref[...] = jnp.zeros(acc_ref.shape, F32)

    def qk(c0, ntile, slot):
        tk = ntile * TM
        k0 = c0 * TM if isinstance(c0, int) else pl.multiple_of(c0 * TM, TM)
        s = _dot(k_ref[0, pl.ds(k0, tk), :], q2_ref[...])
        s_ref[slot, 0:tk, 0:2 * tq] = s
        mx_ref[slot] = jnp.max(s, axis=0, keepdims=True)

    def softmax_pv(c0, ntile, slot):
        s = s_ref[slot, 0:ntile * TM, 0:2 * tq]
        m_prev = m_ref[...]
        m_new = jnp.maximum(m_prev, mx_ref[slot])
        alpha = jnp.exp2(m_prev - m_new)
        p = jnp.exp2(s - m_new)
        l_ref[...] = alpha * l_ref[...] + jnp.sum(p, axis=0, keepdims=True)
        pb = p.astype(BF16)
        vc = jnp.concatenate([v_ref[0, c0 + j] for j in range(ntile)], axis=1)
        acc_ref[...] = alpha * acc_ref[...] + _dot(vc, pb)
        m_ref[...] = m_new

    def finish(qb):
        a = acc_ref[...] * (1.0 / l_ref[...])
        if mode == "diff":
            lv = lam_ref[...]
            lam = (jnp.exp(jnp.sum(lv[0:1] * lv[1:2], keepdims=True))
                   - jnp.exp(jnp.sum(lv[2:3] * lv[3:4], keepdims=True)) + lam_init)
            o = a[:, :tq] - lam * a[:, tq:]
            ms = jnp.mean(o * o, axis=0, keepdims=True)
            o = o * lax.rsqrt(ms + DA_SUBLN_EPS)
            ot = (o.T * g_ref[...]) * (1.0 - lam_init)
        else:
            ot = jnp.concatenate([a[:64, :tq], a[64:, tq:]], axis=0).T
        if isinstance(qb, int):
            o_ref[0, qb * TM:(qb + 1) * TM, :] = ot.astype(BF16)
        else:
            o_ref[0, pl.ds(pl.multiple_of(qb * TM, TM), TM), :] = ot.astype(BF16)

    for qb in range(nct):
        prep_q(qb)
        init()
        for c in range(nct):
            qk(c, 1, 0)
            softmax_pv(c, 1, 0)
        finish(qb)

    n = nt // kvt
    nq = nt - nct
    last = (n - 1) * kvt
    npairs = (n - 1) // 2
    if npairs % unroll:
        unroll = 1

    def tile_core(slot0):
        init()

        def pairs(j, carry):
            for u in range(unroll):
                c = 2 * (j * unroll + u) * kvt
                qk(c + kvt, kvt, 1 - slot0)
                softmax_pv(c, kvt, slot0)
                qk(c + 2 * kvt, kvt, slot0)
                softmax_pv(c + kvt, kvt, 1 - slot0)
            return carry

        lax.fori_loop(0, npairs // unroll + jnp.minimum(pl.program_id(0), 0), pairs, 0)

    def tile_end(qb, slot0, qb_next):
        prep_q(qb_next)
        qk(0, kvt, 1 - slot0)
        softmax_pv(last, kvt, slot0)
        finish(qb)

    prep_q(nct)
    qk(0, kvt, 0)

    def two_tiles(i, carry):
        qb = nct + 2 * i
        tile_core(0)
        tile_end(qb, 0, qb + 1)
        tile_core(1)
        tile_end(qb + 1, 1, jnp.minimum(qb + 2, nt - 1))
        return carry

    lax.fori_loop(0, nq // 2, two_tiles, 0)
    if nq % 2:
        tile_core(0)
        softmax_pv(last, kvt, 0)
        finish(nt - 1)


def _attn_call(lam, g, qt, k, vt, *, mode, nct, lam_init, kvt=KV_TILES, unroll=PAIR_UNROLL):
    B, NT, GD, _ = qt.shape
    T = NT * TM
    G = vt.shape[2] // 128
    dk = GD // G
    body = functools.partial(_attn_body, mode=mode, nct=nct, nt=NT, lam_init=lam_init, kvt=kvt, unroll=unroll)
    return pl.pallas_call(
        body,
        grid=(B, G),
        in_specs=[
            _const_spec(lam.shape), _const_spec(g.shape),
            pl.BlockSpec((1, NT, dk, TM), lambda b, h: (b, 0, h, 0)),
            pl.BlockSpec((1, T, dk), lambda b, h: (b, 0, h)),
            pl.BlockSpec((1, NT, 128, TM), lambda b, h: (b, 0, h, 0)),
        ],
        out_specs=pl.BlockSpec((1, T, 128), lambda b, h: (b, 0, h)),
        out_shape=jax.ShapeDtypeStruct((B, T, G * 128), BF16),
        scratch_shapes=[
            pltpu.VMEM((1, 2 * TM), F32), pltpu.VMEM((1, 2 * TM), F32), pltpu.VMEM((128, 2 * TM), F32),
            pltpu.VMEM((2, kvt * TM, 2 * TM), F32), pltpu.VMEM((2, 1, 2 * TM), F32),
            pltpu.VMEM((dk, 2 * TM), BF16),
        ],
        compiler_params=_cparams(2),
        name="attn_" + mode,
    )(lam, g, qt, k, vt)


def _post_body(*refs, has_gate, has_final):
    x_ref, o_ref = refs[:2]
    nxt = 3 if has_gate else 2
    o = (o_ref[0] * refs[2][0]).astype(BF16) if has_gate else o_ref[0]
    mod_ref, wo_ref, n2_ref, w1_ref, w3_ref, w2_ref = refs[nxt:nxt + 6]
    out_ref = refs[-1]
    m = mod_ref[0]
    x1 = x_ref[0] + m[2:3] * _dot(o, wo_ref[...])
    h = _norm_mod(x1, n2_ref[...], m[3:4], m[4:5]).astype(BF16)
    a = _dot(h, w1_ref[...])
    b = _dot(h, w3_ref[...])
    u = (a * jax.nn.sigmoid(a) * b).astype(BF16)
    x2 = x1 + m[5:6] * _dot(u, w2_ref[...])
    if has_final:
        ms = jnp.mean(x2 * x2, axis=-1, keepdims=True)
        x2 = x2 * lax.rsqrt(ms + NORM_EPS) * refs[nxt + 6][...]
    out_ref[0] = x2


def _post_call(xa, o, gate, mod_i, wo, n2, w1, w3, w2, nct, final_g=None):
    B, T, D = xa.shape
    F = w1.shape[1]
    has_gate = gate is not None
    has_final = final_g is not None
    if has_final:
        x_spec = pl.BlockSpec((1, TM, D), lambda b, t: (b, t + nct, 0))
        mod_spec = pl.BlockSpec((1, 6, D), lambda b, t: (b, 0, 0))
    else:
        x_spec, mod_spec = _tok_specs(B, nct, D)
    ins = [xa, o] + ([gate] if has_gate else []) + [mod_i, wo, n2, w1, w3, w2] + ([final_g] if has_final else [])
    one = pl.Buffered(1)
    specs = [x_spec, x_spec] + ([x_spec] if has_gate else []) + [
        mod_spec,
        pl.BlockSpec((D, D), lambda b, t: (0, 0), pipeline_mode=one),
        _const_spec((1, D)),
        pl.BlockSpec((D, F), lambda b, t: (0, 0), pipeline_mode=one),
        pl.BlockSpec((D, F), lambda b, t: (0, 0), pipeline_mode=one),
        pl.BlockSpec((F, D), lambda b, t: (0, 0), pipeline_mode=one),
    ] + ([_const_spec((1, D))] if has_final else [])
    nt = T // TM - (nct if has_final else 0)
    out_spec = pl.BlockSpec((1, TM, D), lambda b, t: (b, t, 0))
    out_shape = jax.ShapeDtypeStruct((B, nt * TM, D), F32)
    return pl.pallas_call(
        functools.partial(_post_body, has_gate=has_gate, has_final=has_final),
        grid=(B, nt),
        in_specs=specs,
        out_specs=out_spec,
        out_shape=out_shape,
        input_output_aliases={} if has_final else {0: 0},
        compiler_params=_cparams(2),
        name="post_ffn_final" if has_final else "post_ffn",
    )(*ins)


def _mla_pre_body(x_ref, mod_ref, g_ref, wext_ref, gq_ref, gkv_ref, wuqt_ref, wkn_ref, wvt_ref,
                  cq_ref, sq_ref, ck_ref, sk_ref, qt_ref, k_ref, vt_ref, *, scale):
    m = mod_ref[0]
    hb = _norm_mod(x_ref[0], g_ref[...], m[0:1], m[1:2]).astype(BF16)
    down = _dot(hb, wext_ref[...])
    ql = down[:, :MLA_Q_LORA]
    cq_lat = (ql * lax.rsqrt(jnp.mean(ql * ql, axis=-1, keepdims=True) + NORM_EPS) * gq_ref[...]).astype(BF16)
    kl = down[:, MLA_Q_LORA:MLA_Q_LORA + MLA_KV_LORA]
    ckv = (kl * lax.rsqrt(jnp.mean(kl * kl, axis=-1, keepdims=True) + NORM_EPS) * gkv_ref[...]).astype(BF16)
    kr = (down[:, 384:512] * ck_ref[...] + down[:, 512:640] * sk_ref[...]).astype(BF16)
    kn = _dot(ckv, wkn_ref[...])
    npair = kn.shape[1] // 128
    for p in range(npair):
        k_ref[0, :, 256 * p:256 * p + 128] = kn[:, 128 * p:128 * (p + 1)].astype(BF16)
        k_ref[0, :, 256 * p + 128:256 * (p + 1)] = kr
    vt_ref[0, 0] = _dot_nt(wvt_ref[...], ckv).astype(BF16)
    qt = _dot_nt(wuqt_ref[...], cq_lat)
    cq = cq_ref[...]
    sq = sq_ref[...]
    for p in range(npair):
        base = 256 * p
        qt_ref[0, 0, base:base + 128, :] = (qt[base:base + 128] * scale).astype(BF16)
        blk = [qt[base + 128 + 8 * i:base + 136 + 8 * i] for i in range(8)]
        for i in range(8):
            out = blk[i] * cq[8 * i:8 * i + 8] + blk[i ^ 1] * sq[8 * i:8 * i + 8]
            qt_ref[0, 0, base + 128 + 8 * i:base + 136 + 8 * i, :] = out.astype(BF16)
        qt_ref[0, 0, base + 192:base + 256, :] = jnp.zeros((64, qt.shape[1]), BF16)


def _mla_pre_call(xa, mod_i, g1, wext, gq, gkv, wuqt, wkn, wvt, cq, sq, ck, sk, nct, scale):
    B, T, D = xa.shape
    NT = T // TM
    x_spec, mod_spec = _tok_specs(B, nct, D)
    GD = wuqt.shape[0]
    return pl.pallas_call(
        functools.partial(_mla_pre_body, scale=scale),
        grid=(B, NT),
        in_specs=[
            x_spec, mod_spec, _const_spec((1, D)),
            _const_spec(wext.shape), _const_spec(gq.shape), _const_spec(gkv.shape),
            _const_spec(wuqt.shape), _const_spec(wkn.shape), _const_spec(wvt.shape),
            pl.BlockSpec((64, TM), lambda b, t: (0, t)), pl.BlockSpec((64, TM), lambda b, t: (0, t)),
            pl.BlockSpec((TM, 128), lambda b, t: (t, 0)), pl.BlockSpec((TM, 128), lambda b, t: (t, 0)),
        ],
        out_specs=[
            pl.BlockSpec((1, 1, GD, TM), lambda b, t: (b, t, 0, 0)),
            pl.BlockSpec((1, TM, GD), lambda b, t: (b, t, 0)),
            pl.BlockSpec((1, 1, D, TM), lambda b, t: (b, t, 0, 0)),
        ],
        out_shape=[
            jax.ShapeDtypeStruct((B, NT, GD, TM), BF16),
            jax.ShapeDtypeStruct((B, T, GD), BF16),
            jax.ShapeDtypeStruct((B, NT, D, TM), BF16),
        ],
        compiler_params=_cparams(2),
        name="mla_pre",
    )(xa, mod_i, g1, wext, gq, gkv, wuqt, wkn, wvt, cq, sq, ck, sk)


def _rw_pre_body(x_ref, xp_ref, xn_ref, mod_ref, g_ref, mu_ref, wr_ref, wk_ref, wv_ref, w1_ref, w2_ref, w0_ref,
                 a1_ref, a2_ref, a0_ref, g1_ref, g2_ref, r_ref, k_ref, v_ref, wf_ref, wb_ref, af_ref, ab_ref, gt_ref,
                 *, nct, nt):
    t = pl.program_id(1)
    m = mod_ref[0]
    g = g_ref[...]
    h = _norm_mod(x_ref[0], g, m[0:1], m[1:2])
    hp = _norm_mod(xp_ref[0], g, m[0:1], m[1:2])[7:8]
    hn = _norm_mod(xn_ref[0], g, m[0:1], m[1:2])[0:1]
    has_prev = jnp.logical_and(t != 0, t != nct)
    has_next = jnp.logical_and(t != nct - 1, t != nt - 1)
    hp = jnp.where(has_prev, hp, 0.0)
    hn = jnp.where(has_next, hn, 0.0)
    tm = h.shape[0]
    rid = lax.broadcasted_iota(jnp.int32, h.shape, 0)
    h_m1 = jnp.where(rid == 0, hp, pltpu.roll(h, 1, axis=0))
    h_p1 = jnp.where(rid == tm - 1, hn, pltpu.roll(h, tm - 1, axis=0))
    cs = 0.5 * (h_m1 + h_p1) - h
    mu = mu_ref[...]

    def mix(s):
        return (h + cs * mu[s:s + 1]).astype(BF16)

    r_ref[0] = _dot(mix(0), wr_ref[...])
    k_ref[0] = _dot(mix(1), wk_ref[...])
    v_ref[0] = _dot(mix(2), wv_ref[...])
    lw = jnp.tanh(_dot(mix(3), w1_ref[...])).astype(BF16)
    la = _dot(mix(4), a1_ref[...]).astype(BF16)
    for z in range(2):
        wl = w0_ref[z:z + 1] + _dot(lw, w2_ref[z])
        (wf_ref, wb_ref)[z][0] = jnp.exp(-math.exp(-0.5) * jax.nn.sigmoid(wl))
        (af_ref, ab_ref)[z][0] = jax.nn.sigmoid(a0_ref[z:z + 1] + _dot(la, a2_ref[z]))
    gl = jax.nn.sigmoid(_dot(mix(5), g1_ref[...])).astype(BF16)
    gt_ref[0] = _dot(gl, g2_ref[...])


def _rw_pre_call(xa, mod_i, g1n, mu, wr, wk, wv, w1, w2, w0, a1, a2, a0, g1, g2, nct):
    B, T, D = xa.shape
    NT = T // TM
    x_spec, mod_spec = _tok_specs(B, nct, D)
    r8 = TM // 8
    prev_spec = pl.BlockSpec((1, 8, D), lambda b, t: (b, jnp.maximum(t * r8 - 1, 0), 0))
    next_spec = pl.BlockSpec((1, 8, D), lambda b, t: (b, jnp.minimum((t + 1) * r8, T // 8 - 1), 0))
    consts = [g1n, mu, wr, wk, wv, w1, w2, w0, a1, a2, a0, g1, g2]
    tok = jax.ShapeDtypeStruct((B, T, D), F32)
    return pl.pallas_call(
        functools.partial(_rw_pre_body, nct=nct, nt=NT),
        grid=(B, NT),
        in_specs=[x_spec, prev_spec, next_spec, mod_spec] + [_const_spec(c.shape) for c in consts],
        out_specs=[x_spec] * 8,
        out_shape=[tok] * 8,
        compiler_params=_cparams(2),
        name="rw_pre",
    )(xa, xa, xa, mod_i, *consts)


def _rw_scan_body(*refs, tc, reverse, finish):
    r_ref, k_ref, v_ref, w_ref, a_ref, kk_ref, ka_ref = refs[:7]
    y_ref, s_ref = refs[-2:]

    @pl.when(pl.program_id(0) == 0)
    def _():
        s_ref[...] = jnp.zeros(s_ref.shape, F32)

    nv, nk, bh = s_ref.shape
    rowid = lax.broadcasted_iota(jnp.int32, (8, bh), 0)
    low4, low2, low1 = (rowid & 4) == 0, (rowid & 2) == 0, (rowid & 1) == 0

    def fold(x, y, keep, dist):
        u = jnp.where(keep, x, y)
        w = jnp.where(keep, y, x)
        if dist == 4:
            return u + pltpu.roll(w, 4, axis=0)
        return u + jnp.where(keep, pltpu.roll(w, 8 - dist, axis=0), pltpu.roll(w, dist, axis=0))

    def row_sums(parts):
        z = [fold(parts[i], parts[i + 4], low4, 4) for i in range(4)]
        y = [fold(z[i], z[i + 2], low2, 2) for i in range(2)]
        return fold(y[0], y[1], low1, 1)

    def step(i, p):
        te = tc - 1 - i if reverse else i
        r = r_ref[te]
        k = k_ref[te]
        a = a_ref[te]
        kk = k * kk_ref[...]
        nrm = jnp.sqrt(jnp.sum(kk * kk, axis=0, keepdims=True))
        kk = kk / jnp.maximum(nrm, 1e-12)
        pn = p * w_ref[te]
        ip = 1.0 / pn
        at = -kk * p
        bt = kk * a * ip
        kt = k * (1.0 + (a - 1.0) * ka_ref[...]) * ip
        rt = r * pn
        for g in range(nv // 8):
            parts = []
            for j in range(8):
                vi = 8 * g + j
                s = s_ref[vi]
                sa = jnp.sum(s * at, axis=0, keepdims=True)
                vv = v_ref[te, pl.ds(vi, 1), :]
                sn = s + sa * bt + vv * kt
                s_ref[vi] = sn
                parts.append(jnp.sum((sn * rt).reshape(nk // 8, 8, bh), axis=0))
            y_ref[te, 8 * g:8 * g + 8, :] = row_sums(parts)
        return pn

    p_end = lax.fori_loop(0, tc, step, jnp.ones((nk, bh), F32))
    for vi in range(nv):
        s_ref[vi] = s_ref[vi] * p_end

    if finish:
        yo_ref, ao_ref, rk_ref, lg_ref, lb_ref = refs[7:12]
        y = yo_ref[...] + y_ref[...]
        mean = jnp.mean(y, axis=1, keepdims=True)
        yc = y - mean
        var = jnp.mean(yc * yc, axis=1, keepdims=True)
        yn = yc * lax.rsqrt(var + RW_GN_EPS) * lg_ref[...] + lb_ref[...]
        k = k_ref[...]
        ka = ka_ref[...]
        ksum = k * (1.0 + (ao_ref[...] - 1.0) * ka) + k * (1.0 + (a_ref[...] - 1.0) * ka)
        bonus = jnp.sum(r_ref[...] * ksum * rk_ref[...], axis=1, keepdims=True) * v_ref[...]
        y_ref[...] = yn + bonus


def _rw_scan_call(r, k, v, w, a, kk_tab, ka_tab, ncc, reverse, fin=()):
    T, K, BH = r.shape
    tc = SCAN_TC
    NC = T // tc

    def cidx(c):
        return jnp.where(c < ncc, ncc - 1 - c, NC - 1 - (c - ncc)) if reverse else c

    sh_spec = pl.BlockSpec((tc, K, BH), lambda c: (cidx(c), 0, 0))
    tab = _const_spec((K, BH))
    return pl.pallas_call(
        functools.partial(_rw_scan_body, tc=tc, reverse=reverse, finish=bool(fin)),
        grid=(NC,),
        in_specs=[sh_spec] * 5 + [tab, tab] + ([sh_spec, sh_spec, tab, tab, tab] if fin else []),
        out_specs=sh_spec,
        out_shape=jax.ShapeDtypeStruct((T, K, BH), F32),
        scratch_shapes=[pltpu.VMEM((K, K, BH), F32)],
        compiler_params=_cparams(1),
        name="rw_scan_bwd" if reverse else "rw_scan_fwd",
    )(r, k, v, w, a, kk_tab, ka_tab, *fin)


def _rope_tables(n_ctx, n_lat, rot_dim):
    mfreq = rot_dim // 4
    t = jnp.arange(n_lat)
    row = (t // GRID_W).astype(F32)
    col = (t % GRID_W).astype(F32)
    inv = ROPE_BASE ** (-jnp.arange(mfreq, dtype=F32) / mfreq)
    ar = row[:, None] * inv
    ac = col[:, None] * inv
    cos = jnp.concatenate([jnp.cos(ar), jnp.cos(ar), jnp.cos(ac), jnp.cos(ac)], axis=1)
    sin = jnp.concatenate([-jnp.sin(ar), jnp.sin(ar), -jnp.sin(ac), jnp.sin(ac)], axis=1)
    cos = jnp.concatenate([jnp.ones((n_ctx, rot_dim), F32), cos], axis=0)
    sin = jnp.concatenate([jnp.zeros((n_ctx, rot_dim), F32), sin], axis=0)
    return cos, sin


def _swap_perm(n, half):
    idx = jnp.arange(n)
    return idx ^ half


def _to_scan(x, B):
    lead = x.shape[:-3]
    T, D = x.shape[-2:]
    H = D // RW_HEAD
    x = x.reshape(lead + (B, T, H, RW_HEAD))
    nl = len(lead)
    perm = tuple(range(nl)) + (nl + 1, nl + 3, nl, nl + 2)
    return jnp.transpose(x, perm).reshape(lead + (T, RW_HEAD, B * H))


def _head_tab(p, B):
    H = p.size // RW_HEAD
    t = p.reshape(H, RW_HEAD).T
    return jnp.tile(t[:, None, :], (1, B, 1)).reshape(RW_HEAD, B * H).astype(F32)


def kernel(x, c, ctx, c_ctx, ada_w, ada_b, norm1_g, norm2_g, ffn_w1, ffn_w3, ffn_w2, final_norm_g, da_wqkv, da_lambda, da_subln_g, da_wo, rw_mu, rw_wrkv, rw_w0, rw_w1, rw_w2, rw_a0, rw_a1, rw_a2, rw_g1, rw_g2, rw_k_k, rw_k_a, rw_r_k, rw_lnx_g, rw_lnx_b, rw_wo, mla_wdown, mla_q_norm_g, mla_wuq, mla_kv_norm_g, mla_wukv, mla_wo):
    B, N, D = x.shape
    n_ctx = ctx.shape[1]
    depth = ada_w.shape[0]
    assert n_ctx % TM == 0 and N % TM == 0 and n_ctx % SCAN_TC == 0
    nct = n_ctx // TM
    T = n_ctx + N
    assert (T // TM) % KV_TILES == 0 and (T // TM // KV_TILES) % 2 == 1

    xa = jnp.concatenate([ctx, x], axis=1)
    R = -(-(B + 1) // 8) * 8
    cond = jnp.zeros((R, D), F32).at[:B].set(c).at[B].set(c_ctx)
    mod = _ada_call(cond, ada_w, ada_b).reshape(depth, R, 6, D)

    da_cos, da_sin = _rope_tables(n_ctx, N, DA_HEAD_DIM)
    da_scale = (DA_HEAD_DIM ** -0.5) * LOG2E
    da_cq, da_sq = (da_cos * da_scale).T, (da_sin * da_scale).T
    da_ck, da_sk = jnp.tile(da_cos, (1, 2)), jnp.tile(da_sin, (1, 2))
    ml_cos, ml_sin = _rope_tables(n_ctx, N, MLA_ROPE)
    ml_scale = ((MLA_NOPE + MLA_ROPE) ** -0.5) * LOG2E
    ml_cq, ml_sq = jnp.tile((ml_cos * ml_scale).T, (2, 1)), jnp.tile((ml_sin * ml_scale).T, (2, 1))
    zpad = jnp.zeros((T, 64), F32)
    ml_ck = jnp.concatenate([ml_cos, ml_cos, zpad], axis=1)
    ml_sk = jnp.concatenate([ml_sin, ml_sin, zpad], axis=1)

    for i in range(depth):
        kind, j = i % N_MIXERS, i // N_MIXERS
        mod_i = mod[i]
        g1 = norm1_g[i].reshape(1, D)
        gate = None
        if kind == 0:
            wq, wk, wv = jnp.split(da_wqkv[j], 3, axis=1)
            qt, kk, vt = _da_pre_call(xa, mod_i, g1, wq.T.astype(BF16), wk.astype(BF16),
                                      wv.T.astype(BF16), da_cq, da_sq, da_ck, da_sk, nct)
            lam_init = 0.8 - 0.6 * math.exp(-0.3 * i)
            o = _attn_call(da_lambda[j], da_subln_g[j].reshape(1, -1), qt, kk, vt,
                           mode="diff", nct=nct, lam_init=lam_init)
            wo = da_wo[j]
        elif kind == 1:
            H = D // RW_HEAD
            w1c = jnp.concatenate([rw_w1[j, 0], rw_w1[j, 1]], axis=1).astype(BF16)
            a1c = jnp.concatenate([rw_a1[j, 0], rw_a1[j, 1]], axis=1).astype(BF16)
            zl = jnp.zeros_like(rw_w2[j, 0])
            w2p = jnp.stack([jnp.concatenate([rw_w2[j, 0], zl], axis=0),
                             jnp.concatenate([zl, rw_w2[j, 1]], axis=0)]).astype(BF16)
            a2p = jnp.stack([jnp.concatenate([rw_a2[j, 0], zl], axis=0),
                             jnp.concatenate([zl, rw_a2[j, 1]], axis=0)]).astype(BF16)
            gl = rw_g1.shape[2]
            glp = -(-gl // 128) * 128
            g1p = jnp.pad(rw_g1[j], ((0, 0), (0, glp - gl))).astype(BF16)
            g2p = jnp.pad(rw_g2[j], ((0, glp - gl), (0, 0))).astype(BF16)
            *feats, gate = _rw_pre_call(
                xa, mod_i, g1, rw_mu[j], rw_wrkv[j, 0].astype(BF16), rw_wrkv[j, 1].astype(BF16),
                rw_wrkv[j, 2].astype(BF16), w1c, w2p, rw_w0[j], a1c, a2p, rw_a0[j], g1p, g2p, nct)
            rs, ks, vs, wf, wb, af, ab = (_to_scan(u, B) for u in feats)
            ka_tab = _head_tab(rw_k_a[j], B)
            kk_tab = _head_tab(rw_k_k[j], B)
            y0 = _rw_scan_call(rs, ks, vs, wf, af, kk_tab, ka_tab, n_ctx // SCAN_TC, False)
            z = _rw_scan_call(rs, ks, vs, wb, ab, kk_tab, ka_tab, n_ctx // SCAN_TC, True,
                              fin=(y0, af, _head_tab(rw_r_k[j], B), _head_tab(rw_lnx_g[j], B),
                                   _head_tab(rw_lnx_b[j], B)))
            o = jnp.transpose(z.reshape(T, RW_HEAD, B, H), (2, 0, 3, 1)).reshape(B, T, D)
            wo = rw_wo[j]
        else:
            wd = mla_wdown[j]
            nlat = MLA_Q_LORA + MLA_KV_LORA
            kr = wd[:, nlat:]
            krs = kr[:, _swap_perm(MLA_ROPE, 8)]
            zc = jnp.zeros((D, 64), F32)
            wext = jnp.concatenate([wd[:, :nlat], kr, kr, zc, krs, krs, zc], axis=1).astype(BF16)
            wuq = mla_wuq[j].reshape(MLA_Q_LORA, MLA_HEADS // 2, 2, MLA_NOPE + MLA_ROPE)
            qn = wuq[..., :MLA_NOPE].reshape(MLA_Q_LORA, MLA_HEADS // 2, 2 * MLA_NOPE)
            qr = wuq[..., MLA_NOPE:].reshape(MLA_Q_LORA, MLA_HEADS // 2, 2 * MLA_ROPE)
            qz = jnp.zeros((MLA_Q_LORA, MLA_HEADS // 2, 64), F32)
            wuqt = jnp.concatenate([qn, qr, qz], axis=2).reshape(MLA_Q_LORA, -1).T.astype(BF16)
            wukv = mla_wukv[j].reshape(MLA_KV_LORA, MLA_HEADS, MLA_NOPE + MLA_V)
            wkn = wukv[..., :MLA_NOPE].reshape(MLA_KV_LORA, -1).astype(BF16)
            wvt = wukv[..., MLA_NOPE:].reshape(MLA_KV_LORA, -1).T.astype(BF16)
            qt, kk, vt = _mla_pre_call(xa, mod_i, g1, wext, mla_q_norm_g[j].reshape(1, -1),
                                       mla_kv_norm_g[j].reshape(1, -1), wuqt, wkn, wvt,
                                       ml_cq, ml_sq, ml_ck, ml_sk, nct, ml_scale)
            o = _attn_call(jnp.zeros((4, 64), F32), jnp.ones((1, 128), F32), qt, kk, vt,
                           mode="mla", nct=nct, lam_init=0.0)
            wo = mla_wo[j]
        xa = _post_call(xa, o, gate, mod_i, wo.astype(BF16), norm2_g[i].reshape(1, D),
                        ffn_w1[i].astype(BF16), ffn_w3[i].astype(BF16), ffn_w2[i].astype(BF16), nct,
                        final_g=final_norm_g.reshape(1, D) if i == depth - 1 else None)
    return xa
```
